```python
import math
import jax, jax.numpy as jnp
from jax import lax
import numpy as np

D_MODEL = 1024
BATCH = 4
SEQ = 8192
DEPTH = 1

SSM_WIDTH = 512
SSM_GROUP = 16
SSM_GROUPS = SSM_WIDTH // SSM_GROUP
SSM_STATE = 64
DT_MIN = 1e-3
DT_MAX = 1e-1
N_HEADS = 8
HEAD_DIM = 64
ATTN_WIDTH = N_HEADS * HEAD_DIM
MOBA_BLOCK = 256
MOBA_TOPK = 3
QUERY_CHUNK = 32
ROPE_THETA = 500000.0
ROPE_DIM = HEAD_DIM // 4
N_GROUPS = 4
EXPERTS_PER_GROUP = 8
N_EXPERTS = N_GROUPS * EXPERTS_PER_GROUP
TOPK_INNER = 2
D_EXPERT = 128
EPS = 1e-6
IN_WIDTH = SSM_WIDTH + 3 * ATTN_WIDTH + 2 * D_MODEL

kernel_name = "hybrid_s5_moba_hmoe_block"


def _rmsnorm(x, g):
    xf = x.astype(jnp.float32)
    y = xf * lax.rsqrt(jnp.mean(xf * xf, axis=-1, keepdims=True) + EPS)
    return (y * g.astype(jnp.float32)).astype(x.dtype)


def _partial_rope(x, pos):
    half = ROPE_DIM // 2
    inv_freq = ROPE_THETA ** (-jnp.arange(half, dtype=jnp.float32) * 2.0 / ROPE_DIM)
    ang = pos[:, None] * inv_freq[None, :]
    cos, sin = jnp.cos(ang), jnp.sin(ang)
    x1 = x[..., :half]
    x2 = x[..., half:ROPE_DIM]
    rot = jnp.concatenate([x1 * cos - x2 * sin, x2 * cos + x1 * sin], axis=-1)
    return jnp.concatenate([rot, x[..., ROPE_DIM:]], axis=-1)


def _ssm_combine(e1, e2):
    a1r, a1i, b1r, b1i = e1
    a2r, a2i, b2r, b2i = e2
    return (a1r * a2r - a1i * a2i,
            a1r * a2i + a1i * a2r,
            a2r * b1r - a2i * b1i + b2r,
            a2r * b1i + a2i * b1r + b2i)


def _s5_branch(u, lam_re, lam_im, log_dt, b_re, b_im, c_re, c_im, d_skip, w_glu, b_glu):
    Bsz, S, _ = u.shape
    f32 = jnp.float32
    uf = u.astype(f32).reshape(Bsz, S, SSM_GROUPS, SSM_GROUP)
    dt = jnp.exp(log_dt.astype(f32))[:, None]
    lr = jnp.minimum(lam_re.astype(f32), -1e-4)
    li = lam_im.astype(f32)
    mag = jnp.exp(lr * dt)
    ar = mag * jnp.cos(li * dt)
    ai = mag * jnp.sin(li * dt)
    den = lr * lr + li * li
    nr = ar - 1.0
    ni = ai
    cr = (nr * lr + ni * li) / den
    ci = (ni * lr - nr * li) / den
    brf, bif = b_re.astype(f32), b_im.astype(f32)
    bbr = cr[..., None] * brf - ci[..., None] * bif
    bbi = cr[..., None] * bif + ci[..., None] * brf
    bu_r = jnp.einsum('bsgc,gnc->bsgn', uf, bbr)
    bu_i = jnp.einsum('bsgc,gnc->bsgn', uf, bbi)
    a_r = jnp.broadcast_to(ar, bu_r.shape)
    a_i = jnp.broadcast_to(ai, bu_i.shape)
    _, _, st_r, st_i = lax.associative_scan(_ssm_combine, (a_r, a_i, bu_r, bu_i), axis=1)
    y = (jnp.einsum('bsgn,gcn->bsgc', st_r, c_re.astype(f32))
         - jnp.einsum('bsgn,gcn->bsgc', st_i, c_im.astype(f32)))
    y = y.reshape(Bsz, S, SSM_WIDTH) + d_skip.astype(f32) * uf.reshape(Bsz, S, SSM_WIDTH)
    z = jax.nn.gelu(y)
    out = z * jax.nn.sigmoid(z @ w_glu.astype(f32) + b_glu.astype(f32))
    return out.astype(u.dtype)


def _moba_attention(q, k, v):
    Bsz, H, S, Dh = q.shape
    nb = -(-S // MOBA_BLOCK)
    s_pad = nb * MOBA_BLOCK
    padw = ((0, 0), (0, 0), (0, s_pad - S), (0, 0))
    q = jnp.pad(q, padw)
    k = jnp.pad(k, padw)
    v = jnp.pad(v, padw)
    kb = k.reshape(Bsz, H, nb, MOBA_BLOCK, Dh)
    vb = v.reshape(Bsz, H, nb, MOBA_BLOCK, Dh)
    kmean = jnp.mean(kb, axis=3)
    qblk = jnp.arange(s_pad) // MOBA_BLOCK
    scores = jnp.einsum('bhsd,bhnd->bhsn', q, kmean)
    past = jnp.arange(nb)[None, :] < qblk[:, None]
    scores = jnp.where(past, scores, -jnp.inf)
    k_sel = min(MOBA_TOPK, nb)
    _, top_idx = lax.top_k(scores, k_sel)
    valid = top_idx < qblk[:, None]
    own = jnp.broadcast_to(qblk[:, None], (Bsz, H, s_pad, 1)).astype(top_idx.dtype)
    sel = jnp.concatenate([top_idx, own], axis=-1)
    sel_valid = jnp.concatenate([valid, jnp.ones((Bsz, H, s_pad, 1), dtype=bool)], axis=-1)
    m = k_sel + 1
    scale = HEAD_DIM ** -0.5
    gather = jax.vmap(jax.vmap(lambda blocks, idx: blocks[idx]))
    key_off = jnp.arange(MOBA_BLOCK)

    def chunk(ci):
        start = ci * QUERY_CHUNK
        qc = lax.dynamic_slice_in_dim(q, start, QUERY_CHUNK, axis=2)
        sc = lax.dynamic_slice_in_dim(sel, start, QUERY_CHUNK, axis=2)
        vc = lax.dynamic_slice_in_dim(sel_valid, start, QUERY_CHUNK, axis=2)
        kg = gather(kb, sc)
        vg = gather(vb, sc)
        logits = jnp.einsum('bhqd,bhqmld->bhqml', qc, kg) * scale
        qpos = start + jnp.arange(QUERY_CHUNK)
        kpos = sc[..., None] * MOBA_BLOCK + key_off
        allowed = vc[..., None] & (kpos <= qpos[:, None, None])
        logits = jnp.where(allowed, logits, -jnp.inf)
        p = jax.nn.softmax(logits.reshape(Bsz, H, QUERY_CHUNK, m * MOBA_BLOCK), axis=-1)
        p = p.reshape(Bsz, H, QUERY_CHUNK, m, MOBA_BLOCK)
        return jnp.einsum('bhqml,bhqmld->bhqd', p, vg)

    outs = lax.map(chunk, jnp.arange(s_pad // QUERY_CHUNK))
    out = outs.transpose(1, 2, 0, 3, 4).reshape(Bsz, H, s_pad, Dh)
    return out[:, :, :S]


def _hier_moe(h, w_rg, b_rg, w_re, b_re, w_gate, w_up, w_down):
    Bsz, S, D = h.shape
    f32 = jnp.float32
    t = h.reshape(-1, D)
    g_prob = jax.nn.softmax((t @ w_rg).astype(f32) + b_rg.astype(f32), axis=-1)
    g_p, g_idx = lax.top_k(g_prob, 1)
    e_logits = ((t @ w_re).astype(f32) + b_re.astype(f32)).reshape(-1, N_GROUPS, EXPERTS_PER_GROUP)
    g_onehot = jax.nn.one_hot(g_idx[:, 0], N_GROUPS, dtype=f32)
    e_in = jnp.einsum('tge,tg->te', e_logits, g_onehot)
    e_val, e_idx = lax.top_k(e_in, TOPK_INNER)
    e_w = jax.nn.softmax(e_val, axis=-1) * g_p
    flat = g_idx * EXPERTS_PER_GROUP + e_idx
    comb = jnp.sum(jax.nn.one_hot(flat, N_EXPERTS, dtype=f32) * e_w[..., None], axis=1)
    hg = jnp.einsum('td,edf->tef', t, w_gate)
    hu = jnp.einsum('td,edf->tef', t, w_up)
    act = jax.nn.silu(hg) * hu * comb[:, :, None].astype(t.dtype)
    y = jnp.einsum('tef,efd->td', act, w_down)
    return y.reshape(Bsz, S, D)


def setup_inputs(seed: int = 0) -> dict:
    key = jax.random.key(seed)
    ks = jax.random.split(key, 32)
    f32 = jnp.float32
    nrm = lambda k, shape, s: jax.random.normal(k, shape, f32) * s
    L = DEPTH
    lam_im = (jnp.pi * jnp.arange(SSM_STATE, dtype=f32))[None, None, :] + nrm(ks[4], (L, SSM_GROUPS, SSM_STATE), 0.01)
    return {
        "x": nrm(ks[0], (BATCH, SEQ, D_MODEL), 1.0),
        "norm1_g": 1.0 + nrm(ks[1], (L, D_MODEL), 0.01),
        "w_in": nrm(ks[2], (L, D_MODEL, IN_WIDTH), D_MODEL ** -0.5),
        "lam_re": -0.5 + nrm(ks[3], (L, SSM_GROUPS, SSM_STATE), 0.01),
        "lam_im": lam_im,
        "log_dt": jax.random.uniform(ks[5], (L, SSM_GROUPS), f32, math.log(DT_MIN), math.log(DT_MAX)),
        "ssm_b_re": nrm(ks[6], (L, SSM_GROUPS, SSM_STATE, SSM_GROUP), (2 * SSM_GROUP) ** -0.5),
        "ssm_b_im": nrm(ks[7], (L, SSM_GROUPS, SSM_STATE, SSM_GROUP), (2 * SSM_GROUP) ** -0.5),
        "ssm_c_re": nrm(ks[8], (L, SSM_GROUPS, SSM_GROUP, SSM_STATE), 0.5),
        "ssm_c_im": nrm(ks[9], (L, SSM_GROUPS, SSM_GROUP, SSM_STATE), 0.5),
        "ssm_d": nrm(ks[10], (L, SSM_WIDTH), 1.0),
        "w_glu": nrm(ks[11], (L, SSM_WIDTH, SSM_WIDTH), SSM_WIDTH ** -0.5),
        "b_glu": nrm(ks[12], (L, SSM_WIDTH), 0.01),
        "q_norm_g": 1.0 + nrm(ks[13], (L, HEAD_DIM), 0.01),
        "k_norm_g": 1.0 + nrm(ks[14], (L, HEAD_DIM), 0.01),
        "w_proj_ssm": nrm(ks[15], (L, SSM_WIDTH, D_MODEL), SSM_WIDTH ** -0.5),
        "w_proj_attn": nrm(ks[16], (L, ATTN_WIDTH, D_MODEL), ATTN_WIDTH ** -0.5),
        "w_out": nrm(ks[17], (L, D_MODEL, D_MODEL), D_MODEL ** -0.5),
        "norm2_g": 1.0 + nrm(ks[18], (L, D_MODEL), 0.01),
        "w_router_group": nrm(ks[19], (L, D_MODEL, N_GROUPS), D_MODEL ** -0.5),
        "b_router_group": nrm(ks[20], (L, N_GROUPS), 0.01),
        "w_router_expert": nrm(ks[21], (L, D_MODEL, N_EXPERTS), D_MODEL ** -0.5),
        "b_router_expert": nrm(ks[22], (L, N_EXPERTS), 0.01),
        "w_gate": nrm(ks[23], (L, N_EXPERTS, D_MODEL, D_EXPERT), D_MODEL ** -0.5),
        "w_up": nrm(ks[24], (L, N_EXPERTS, D_MODEL, D_EXPERT), D_MODEL ** -0.5),
        "w_down": nrm(ks[25], (L, N_EXPERTS, D_EXPERT, D_MODEL), D_EXPERT ** -0.5),
    }


def reference(x, norm1_g, w_in, lam_re, lam_im, log_dt, ssm_b_re, ssm_b_im, ssm_c_re, ssm_c_im,
              ssm_d, w_glu, b_glu, q_norm_g, k_norm_g, w_proj_ssm, w_proj_attn, w_out, norm2_g,
              w_router_group, b_router_group, w_router_expert, b_router_expert, w_gate, w_up, w_down):
    Bsz, S, D = x.shape
    pos = jnp.arange(S, dtype=jnp.float32)
    offs = [SSM_WIDTH, SSM_WIDTH + ATTN_WIDTH, SSM_WIDTH + 2 * ATTN_WIDTH,
            SSM_WIDTH + 3 * ATTN_WIDTH, SSM_WIDTH + 3 * ATTN_WIDTH + D_MODEL]
    for l in range(DEPTH):
        h = _rmsnorm(x, norm1_g[l])
        proj = h @ w_in[l]
        u, q, k, v, g_ssm, g_attn = jnp.split(proj, offs, axis=-1)
        y_ssm = _s5_branch(u, lam_re[l], lam_im[l], log_dt[l], ssm_b_re[l], ssm_b_im[l],
                           ssm_c_re[l], ssm_c_im[l], ssm_d[l], w_glu[l], b_glu[l]) @ w_proj_ssm[l]
        heads = lambda t: t.astype(jnp.float32).reshape(Bsz, S, N_HEADS, HEAD_DIM).transpose(0, 2, 1, 3)
        qh = _partial_rope(_rmsnorm(heads(q), q_norm_g[l]), pos)
        kh = _partial_rope(_rmsnorm(heads(k), k_norm_g[l]), pos)
        vh = heads(v)
        att = _moba_attention(qh, kh, vh)
        att = att.transpose(0, 2, 1, 3).reshape(Bsz, S, ATTN_WIDTH).astype(x.dtype)
        y_attn = att @ w_proj_attn[l]
        merged = jax.nn.sigmoid(g_ssm) * y_ssm + jax.nn.sigmoid(g_attn) * y_attn
        x = x + merged @ w_out[l]
        h2 = _rmsnorm(x, norm2_g[l])
        x = x + _hier_moe(h2, w_router_group[l], b_router_group[l], w_router_expert[l],
                          b_router_expert[l], w_gate[l], w_up[l], w_down[l])
    return x
```

```python
import functools
import math

import jax
import jax.numpy as jnp
from jax import lax
from jax.experimental import pallas as pl
from jax.experimental.pallas import tpu as pltpu

D_MODEL = 1024
SSM_WIDTH = 512
SSM_GROUP = 16
SSM_GROUPS = SSM_WIDTH // SSM_GROUP
SSM_STATE = 64
N_HEADS = 8
HEAD_DIM = 64
ATTN_WIDTH = N_HEADS * HEAD_DIM
MOBA_BLOCK = 256
MOBA_TOPK = 3
ROPE_THETA = 500000.0
ROPE_DIM = HEAD_DIM // 4
N_GROUPS = 4
EXPERTS_PER_GROUP = 8
N_EXPERTS = N_GROUPS * EXPERTS_PER_GROUP
D_EXPERT = 128
EPS = 1e-6
IN_WIDTH = SSM_WIDTH + 3 * ATTN_WIDTH + 2 * D_MODEL

LANES = 128
SUBLANES = 8
VMEM_LIMIT = 56 * 1024 * 1024

SSM_CHUNK = 64
NEG_BIG = -1e30

_NT = (((1,), (1,)), ((), ()))


def _params(*sem):
    return pltpu.CompilerParams(dimension_semantics=sem, vmem_limit_bytes=VMEM_LIMIT)


def _in_proj_kernel(x_ref, g_ref, w_ref, o_ref):
    x = x_ref[...]
    ms = jnp.mean(x * x, axis=-1, keepdims=True)
    h = (x * lax.rsqrt(ms + EPS) * g_ref[...]).astype(jnp.bfloat16)
    n_out = o_ref.shape[1]
    step = 512
    for c in range(n_out // step):
        o_ref[:, c * step:(c + 1) * step] = jnp.dot(
            h, w_ref[:, c * step:(c + 1) * step], preferred_element_type=jnp.float32)


def _in_proj(x2, g, w_bf, tm=256):
    T, D = x2.shape
    N = w_bf.shape[1]
    return pl.pallas_call(
        _in_proj_kernel,
        grid=(T // tm,),
        in_specs=[pl.BlockSpec((tm, D), lambda i: (i, 0)),
                  pl.BlockSpec((1, D), lambda i: (0, 0)),
                  pl.BlockSpec((D, N), lambda i: (0, 0))],
        out_specs=pl.BlockSpec((tm, N), lambda i: (i, 0)),
        out_shape=jax.ShapeDtypeStruct((T, N), jnp.float32),
        compiler_params=_params("parallel"),
        name="in_proj",
    )(x2, g.reshape(1, D), w_bf)


def _head_norm_rope(x, g, ct, sa, sb):
    ts = x.shape[0]
    lane = lax.broadcasted_iota(jnp.int32, (ts, LANES), 1)
    lo = lane < HEAD_DIM
    outs = []
    for t in range(x.shape[1] // LANES):
        xt = x[:, t * LANES:(t + 1) * LANES]
        sq = xt * xt
        s_lo = jnp.sum(jnp.where(lo, sq, 0.0), axis=-1, keepdims=True)
        s_hi = jnp.sum(jnp.where(lo, 0.0, sq), axis=-1, keepdims=True)
        ms = jnp.where(lo, s_lo, s_hi) * (1.0 / HEAD_DIM)
        y = xt * lax.rsqrt(ms + EPS) * g
        y = y * ct + pltpu.roll(y, LANES - ROPE_DIM // 2, 1) * sa + pltpu.roll(y, ROPE_DIM // 2, 1) * sb
        outs.append(y)
    return jnp.concatenate(outs, axis=1)


def _qk_prep_kernel(q_ref, k_ref, v_ref, gq_ref, gk_ref, ct_ref, sa_ref, sb_ref,
                    qo_ref, ko_ref, vo_ref, km_ref):
    ct, sa, sb = ct_ref[...], sa_ref[...], sb_ref[...]
    qn = _head_norm_rope(q_ref[...], gq_ref[...], ct, sa, sb)
    kn = _head_norm_rope(k_ref[...], gk_ref[...], ct, sa, sb)
    qo_ref[0] = qn
    ko_ref[0] = kn.astype(jnp.bfloat16)
    vo_ref[0] = v_ref[...].astype(jnp.bfloat16)
    km_ref[0, 0] = jnp.mean(kn, axis=0, keepdims=True)


def _rope_tables(S):
    half = ROPE_DIM // 2
    inv_freq = ROPE_THETA ** (-jnp.arange(half, dtype=jnp.float32) * 2.0 / ROPE_DIM)
    ang = jnp.arange(S, dtype=jnp.float32)[:, None] * inv_freq[None, :]
    cos, sin = jnp.cos(ang), jnp.sin(ang)
    ones = jnp.ones((S, HEAD_DIM - ROPE_DIM), jnp.float32)
    zeros = jnp.zeros((S, HEAD_DIM - ROPE_DIM), jnp.float32)
    zh = jnp.zeros((S, half), jnp.float32)
    ct = jnp.concatenate([cos, cos, ones], axis=1)
    sa = jnp.concatenate([-sin, zh, zeros], axis=1)
    sb = jnp.concatenate([zh, sin, zeros], axis=1)
    tile = lambda t: jnp.concatenate([t, t], axis=1)
    return tile(ct), tile(sa), tile(sb)


def _qk_prep(proj, gq, gk, B, S):
    ts = MOBA_BLOCK
    nb = S // ts
    W = ATTN_WIDTH
    ct, sa, sb = _rope_tables(S)
    g128 = lambda g: jnp.concatenate([g, g]).reshape(1, LANES)
    row = lambda b, s: b * nb + s
    tab = pl.BlockSpec((ts, LANES), lambda b, s: (s, 0))
    out3 = pl.BlockSpec((1, ts, W), lambda b, s: (b, s, 0))
    return pl.pallas_call(
        _qk_prep_kernel,
        grid=(B, nb),
        in_specs=[pl.BlockSpec((ts, W), lambda b, s: (row(b, s), 1)),
                  pl.BlockSpec((ts, W), lambda b, s: (row(b, s), 2)),
                  pl.BlockSpec((ts, W), lambda b, s: (row(b, s), 3)),
                  pl.BlockSpec((1, LANES), lambda b, s: (0, 0)),
                  pl.BlockSpec((1, LANES), lambda b, s: (0, 0)),
                  tab, tab, tab],
        out_specs=[out3, out3, out3,
                   pl.BlockSpec((1, 1, 1, W), lambda b, s: (b, s, 0, 0))],
        out_shape=[jax.ShapeDtypeStruct((B, S, W), jnp.float32),
                   jax.ShapeDtypeStruct((B, S, W), jnp.bfloat16),
                   jax.ShapeDtypeStruct((B, S, W), jnp.bfloat16),
                   jax.ShapeDtypeStruct((B, nb, 1, W), jnp.float32)],
        compiler_params=_params("parallel", "parallel"),
        name="qk_prep",
    )(proj, proj, proj, g128(gq), g128(gk), ct, sa, sb)


def _moba_kernel(q_ref, k_ref, v_ref, km_ref, oh_ref, o_ref):
    i = pl.program_id(2)
    tq = q_ref.shape[1]
    nb = km_ref.shape[1]
    q = q_ref[0]
    km = km_ref[0]
    lane = lax.broadcasted_iota(jnp.int32, (tq, LANES), 1)
    blk = lax.broadcasted_iota(jnp.int32, (nb, tq), 0)
    past = blk < i
    scale = HEAD_DIM ** -0.5

    q_aug = []
    for h in range(2):
        hmask = (lane < HEAD_DIM) if h == 0 else (lane >= HEAD_DIM)
        qh = jnp.where(hmask, q, 0.0)
        s = lax.dot_general(km, qh, _NT, precision=lax.Precision.HIGHEST,
                            preferred_element_type=jnp.float32)
        s = jnp.where(past, s, -jnp.inf)
        sel = jnp.zeros((nb, tq), jnp.bool_)
        for _ in range(MOBA_TOPK):
            mx = jnp.max(s, axis=0, keepdims=True)
            first = jnp.min(jnp.where(s == mx, blk, nb), axis=0, keepdims=True)
            pick = blk == first
            sel = jnp.logical_or(sel, jnp.logical_and(pick, past))
            s = jnp.where(pick, -jnp.inf, s)
        bias_t = jnp.where(sel, 0.0, NEG_BIG)
        bias_t = jnp.concatenate([bias_t, jnp.zeros((LANES - nb, tq), jnp.float32)], axis=0)
        bias = bias_t.T.astype(jnp.bfloat16)
        q_aug.append(jnp.concatenate([(qh * scale).astype(jnp.bfloat16), bias], axis=1))

    off_d = pl.multiple_of(i * tq, tq)
    kd = k_ref[0, pl.ds(off_d, tq), :]
    vd = v_ref[0, pl.ds(off_d, tq), :]
    r = lax.broadcasted_iota(jnp.int32, (tq, tq), 0)
    c = lax.broadcasted_iota(jnp.int32, (tq, tq), 1)
    causal = c <= r
    carry = []
    for h in range(2):
        s = lax.dot_general(q_aug[h][:, :LANES], kd, _NT, preferred_element_type=jnp.float32)
        s = jnp.where(causal, s, -jnp.inf)
        m = jnp.max(s, axis=-1, keepdims=True)
        p = jnp.exp(s - m)
        l = jnp.sum(p, axis=-1, keepdims=True)
        acc = jnp.dot(p.astype(jnp.bfloat16), vd, preferred_element_type=jnp.float32)
        carry += [m, l, acc]

    def body(n, carry):
        off = pl.multiple_of(n * tq, tq)
        k_aug = jnp.concatenate([k_ref[0, pl.ds(off, tq), :], oh_ref[pl.ds(off, tq), :]], axis=1)
        vn = v_ref[0, pl.ds(off, tq), :]
        out = []
        for h in range(2):
            m, l, acc = carry[3 * h:3 * h + 3]
            s = lax.dot_general(q_aug[h], k_aug, _NT, preferred_element_type=jnp.float32)
            m_new = jnp.maximum(m, jnp.max(s, axis=-1, keepdims=True))
            alpha = jnp.exp(m - m_new)
            p = jnp.exp(s - m_new)
            l = alpha * l + jnp.sum(p, axis=-1, keepdims=True)
            acc = alpha * acc + jnp.dot(p.astype(jnp.bfloat16), vn, preferred_element_type=jnp.float32)
            out += [m_new, l, acc]
        return tuple(out)

    carry = lax.fori_loop(0, i, body, tuple(carry))
    o0 = carry[2] / carry[1]
    o1 = carry[5] / carry[4]
    o_ref[0] = jnp.where(lane < HEAD_DIM, o0, o1).astype(o_ref.dtype)


def _moba(q, k_bf, v_bf, kmean):
    B, S, W = q.shape
    tq = MOBA_BLOCK
    nb = S // tq
    onehot = (jnp.arange(S)[:, None] // tq == jnp.arange(LANES)[None, :]).astype(jnp.bfloat16)
    kv = pl.BlockSpec((1, S, LANES), lambda b, hp, i: (b, 0, hp))
    qo = pl.BlockSpec((1, tq, LANES), lambda b, hp, i: (b, i, hp))
    return pl.pallas_call(
        _moba_kernel,
        grid=(B, W // LANES, nb),
        in_specs=[qo, kv, kv,
                  pl.BlockSpec((1, nb, LANES), lambda b, hp, i: (b, 0, hp)),
                  pl.BlockSpec((S, LANES), lambda b, hp, i: (0, 0))],
        out_specs=qo,
        out_shape=jax.ShapeDtypeStruct((B, S, W), jnp.bfloat16),
        compiler_params=_params("parallel", "parallel", "arbitrary"),
        name="moba",
    )(q, k_bf, v_bf, kmean, onehot)


def _ssm_weights(lam_re, lam_im, log_dt, b_re, b_im, c_re, c_im):
    L = SSM_CHUNK
    f32 = jnp.float32
    dt = jnp.exp(log_dt.astype(f32))[:, None]
    lr = jnp.minimum(lam_re.astype(f32), -1e-4)
    li = lam_im.astype(f32)
    mag = jnp.exp(lr * dt)
    ar = mag * jnp.cos(li * dt)
    ai = mag * jnp.sin(li * dt)
    den = lr * lr + li * li
    nr, ni = ar - 1.0, ai
    cr = (nr * lr + ni * li) / den
    ci = (ni * lr - nr * li) / den
    brf, bif = b_re.astype(f32), b_im.astype(f32)
    bbr = cr[..., None] * brf - ci[..., None] * bif
    bbi = cr[..., None] * bif + ci[..., None] * brf
    tau = jnp.arange(L + 1, dtype=f32)[:, None, None]
    pmag = jnp.exp(tau * (lr * dt)[None])
    pr = pmag * jnp.cos(tau * (li * dt)[None])
    pi = pmag * jnp.sin(tau * (li * dt)[None])
    cre, cim = c_re.astype(f32), c_im.astype(f32)
    hi = lax.Precision.HIGHEST
    car = cre[None] * pr[:, :, None, :] - cim[None] * pi[:, :, None, :]
    cai = cre[None] * pi[:, :, None, :] + cim[None] * pr[:, :, None, :]
    kern = (jnp.einsum('tgon,gni->tgoi', car[:L], bbr, precision=hi)
            - jnp.einsum('tgon,gni->tgoi', cai[:L], bbi, precision=hi))
    s_idx = jnp.arange(L)[:, None]
    t_idx = jnp.arange(L)[None, :]
    lag = t_idx - s_idx
    toep = jnp.where((lag >= 0)[:, :, None, None, None],
                     kern[jnp.clip(lag, 0, L - 1)], 0.0)
    toep = toep.transpose(2, 0, 4, 1, 3).reshape(SSM_GROUPS, L * SSM_GROUP, L * SSM_GROUP)
    rev_r, rev_i = pr[L - 1::-1][:L], pi[L - 1::-1][:L]
    st_r = rev_r[:, :, :, None] * bbr[None] - rev_i[:, :, :, None] * bbi[None]
    st_i = rev_r[:, :, :, None] * bbi[None] + rev_i[:, :, :, None] * bbr[None]
    w_st = jnp.concatenate([st_r, st_i], axis=2)
    w_st = w_st.transpose(1, 0, 3, 2).reshape(SSM_GROUPS, L * SSM_GROUP, 2 * SSM_STATE)
    w_out = jnp.concatenate([car[1:], -cai[1:]], axis=3)
    w_out = w_out.transpose(1, 3, 0, 2).reshape(SSM_GROUPS, 2 * SSM_STATE, L * SSM_GROUP)
    a_chunk = jnp.concatenate([pr[L], pi[L]], axis=1)
    return toep, w_st, w_out, a_chunk


def _ssm_local_kernel(u_ref, w_ref, o_ref):
    o_ref[0] = jnp.dot(u_ref[0], w_ref[0], preferred_element_type=jnp.float32)


def _ssm_scan_kernel(s_ref, p_ref, q_ref, o_ref):
    n_chunks = s_ref.shape[1]
    pm, qm = p_ref[...], q_ref[...]

    def body(c, h):
        o_ref[0, c] = h
        return h * pm + pltpu.roll(h, SSM_STATE, 1) * qm + s_ref[0, c]

    lax.fori_loop(0, n_chunks, body, jnp.zeros(pm.shape, jnp.float32))


def _ssm_out_kernel(u_ref, h_ref, t_ref, w_ref, o_ref):
    y = jnp.dot(u_ref[0], t_ref[0], preferred_element_type=jnp.float32)
    y += jnp.dot(h_ref[0].astype(jnp.bfloat16), w_ref[0], preferred_element_type=jnp.float32)
    o_ref[0] = y


def _ssm(u, toep, w_st, w_out, a_chunk, B, S):
    L, G, C16 = SSM_CHUNK, SSM_GROUPS, SSM_GROUP
    nc = S // L
    R = B * nc
    K = L * C16
    N2 = 2 * SSM_STATE
    bf = jnp.bfloat16
    ug = u.reshape(R, L, G, C16).transpose(2, 0, 1, 3).reshape(G, R, K).astype(bf)
    s_loc = pl.pallas_call(
        _ssm_local_kernel,
        grid=(G,),
        in_specs=[pl.BlockSpec((1, R, K), lambda g: (g, 0, 0)),
                  pl.BlockSpec((1, K, N2), lambda g: (g, 0, 0))],
        out_specs=pl.BlockSpec((1, R, N2), lambda g: (g, 0, 0)),
        out_shape=jax.ShapeDtypeStruct((G, R, N2), jnp.float32),
        compiler_params=_params("parallel"),
        name="ssm_local",
    )(ug, w_st.astype(bf))
    s_bc = s_loc.reshape(G, B, nc, N2).transpose(1, 2, 0, 3)
    ar, ai = a_chunk[:, :SSM_STATE], a_chunk[:, SSM_STATE:]
    pm = jnp.concatenate([ar, ar], axis=1)
    qm = jnp.concatenate([-ai, ai], axis=1)
    h_in = pl.pallas_call(
        _ssm_scan_kernel,
        grid=(B,),
        in_specs=[pl.BlockSpec((1, nc, G, N2), lambda b: (b, 0, 0, 0)),
                  pl.BlockSpec((G, N2), lambda b: (0, 0)),
                  pl.BlockSpec((G, N2), lambda b: (0, 0))],
        out_specs=pl.BlockSpec((1, nc, G, N2), lambda b: (b, 0, 0, 0)),
        out_shape=jax.ShapeDtypeStruct((B, nc, G, N2), jnp.float32),
        compiler_params=_params("parallel"),
        name="ssm_scan",
    )(s_bc, pm, qm)
    h_g = h_in.transpose(2, 0, 1, 3).reshape(G, R, N2)
    y_g = pl.pallas_call(
        _ssm_out_kernel,
        grid=(G,),
        in_specs=[pl.BlockSpec((1, R, K), lambda g: (g, 0, 0)),
                  pl.BlockSpec((1, R, N2), lambda g: (g, 0, 0)),
                  pl.BlockSpec((1, K, K), lambda g: (g, 0, 0)),
                  pl.BlockSpec((1, N2, K), lambda g: (g, 0, 0))],
        out_specs=pl.BlockSpec((1, R, K), lambda g: (g, 0, 0)),
        out_shape=jax.ShapeDtypeStruct((G, R, K), jnp.float32),
        compiler_params=_params("parallel"),
        name="ssm_out",
    )(ug, h_g, toep.astype(bf), w_out.astype(bf))
    return y_g.reshape(G, R, L, C16).transpose(1, 2, 0, 3).reshape(B * S, SSM_WIDTH)


def _route(logits):
    tm = logits.shape[0]
    lane = lax.broadcasted_iota(jnp.int32, (tm, LANES), 1)
    is_g = jnp.logical_and(lane >= N_EXPERTS, lane < N_EXPERTS + N_GROUPS)
    gl = jnp.where(is_g, logits, -jnp.inf)
    gmax = jnp.max(gl, axis=-1, keepdims=True)
    ge = jnp.exp(gl - gmax)
    gp = ge / jnp.sum(ge, axis=-1, keepdims=True)
    g_p = jnp.max(gp, axis=-1, keepdims=True)
    g_lane = jnp.min(jnp.where(jnp.logical_and(is_g, gp == g_p), lane, LANES), axis=-1, keepdims=True)
    g_idx = g_lane - N_EXPERTS
    grp_of_lane = lax.shift_right_logical(lane, int(math.log2(EXPERTS_PER_GROUP)))
    in_grp = jnp.logical_and(lane < N_EXPERTS, grp_of_lane == g_idx)
    el = jnp.where(in_grp, logits, -jnp.inf)
    v1 = jnp.max(el, axis=-1, keepdims=True)
    i1 = jnp.min(jnp.where(el == v1, lane, LANES), axis=-1, keepdims=True)
    el2 = jnp.where(lane == i1, -jnp.inf, el)
    v2 = jnp.max(el2, axis=-1, keepdims=True)
    i2 = jnp.min(jnp.where(el2 == v2, lane, LANES), axis=-1, keepdims=True)
    e2 = jnp.exp(v2 - v1)
    den = 1.0 + e2
    w1 = (1.0 / den) * g_p
    w2 = (e2 / den) * g_p
    return jnp.where(lane == i1, w1, jnp.where(lane == i2, w2, 0.0))


def _merge_kernel(y_ref, u_ref, att_ref, gs_ref, ga_ref, x_ref, d_ref, wglu_ref, bglu_ref,
                  wps_ref, wpa_ref, wo_ref, g2_ref, wr_ref, br_ref,
                  x1_ref, h2_ref, comb_ref):
    bf = jnp.bfloat16
    f32 = jnp.float32
    y = y_ref[...] + d_ref[...] * u_ref[...]
    z = jax.nn.gelu(y)
    gate = jnp.dot(z.astype(bf), wglu_ref[...], preferred_element_type=f32) + bglu_ref[...]
    s5 = z * jax.nn.sigmoid(gate)
    y_ssm = jnp.dot(s5.astype(bf), wps_ref[...], preferred_element_type=f32)
    y_att = jnp.dot(att_ref[...], wpa_ref[...], preferred_element_type=f32)
    merged = jax.nn.sigmoid(gs_ref[...]) * y_ssm + jax.nn.sigmoid(ga_ref[...]) * y_att
    x1 = x_ref[...] + jnp.dot(merged.astype(bf), wo_ref[...], preferred_element_type=f32)
    x1_ref[...] = x1
    ms = jnp.mean(x1 * x1, axis=-1, keepdims=True)
    h2 = x1 * lax.rsqrt(ms + EPS) * g2_ref[...]
    h2_ref[...] = h2.astype(bf)
    logits = jnp.dot(h2, wr_ref[...], precision=lax.Precision.HIGHEST,
                     preferred_element_type=f32) + br_ref[...]
    comb_ref[...] = _route(logits)


def _merge(y_raw, proj, att, x2, d, w_glu, b_glu, w_ps, w_pa, w_o, g2, w_r, b_r, tm=256):
    T, D = x2.shape
    W = SSM_WIDTH
    row = lambda shape, col: pl.BlockSpec(shape, lambda i: (i, col))
    full = lambda a: pl.BlockSpec(a.shape, lambda i: (0,) * a.ndim)
    consts = [d.reshape(1, W), w_glu, b_glu.reshape(1, W), w_ps, w_pa, w_o, g2.reshape(1, D), w_r, b_r]
    return pl.pallas_call(
        _merge_kernel,
        grid=(T // tm,),
        in_specs=[row((tm, W), 0), row((tm, W), 0), row((tm, W), 0),
                  row((tm, D), 2), row((tm, D), 3), row((tm, D), 0)] + [full(c) for c in consts],
        out_specs=[row((tm, D), 0), row((tm, D), 0), row((tm, LANES), 0)],
        out_shape=[jax.ShapeDtypeStruct((T, D), jnp.float32),
                   jax.ShapeDtypeStruct((T, D), jnp.bfloat16),
                   jax.ShapeDtypeStruct((T, LANES), jnp.float32)],
        compiler_params=_params("parallel"),
        name="merge",
    )(y_raw, proj, att, proj, proj, x2, *consts)


def _moe_kernel(h_ref, comb_ref, x_ref, wg_ref, wu_ref, wd_ref, o_ref, act_ref):
    f32 = jnp.float32
    h = h_ref[...]
    comb = comb_ref[...]
    step = 4 * D_EXPERT
    for c in range(N_EXPERTS * D_EXPERT // step):
        hg = jnp.dot(h, wg_ref[:, c * step:(c + 1) * step], preferred_element_type=f32)
        hu = jnp.dot(h, wu_ref[:, c * step:(c + 1) * step], preferred_element_type=f32)
        a = jax.nn.silu(hg) * hu
        for j in range(step // D_EXPERT):
            e = c * (step // D_EXPERT) + j
            act_ref[:, e * D_EXPERT:(e + 1) * D_EXPERT] = (
                a[:, j * D_EXPERT:(j + 1) * D_EXPERT] * comb[:, e:e + 1]).astype(act_ref.dtype)
    o_ref[...] = x_ref[...] + jnp.dot(act_ref[...], wd_ref[...], preferred_element_type=f32)


def _moe(h2, comb, x1, wg, wu, wd, tm=512):
    T, D = x1.shape
    F = N_EXPERTS * D_EXPERT
    row = lambda w: pl.BlockSpec((tm, w), lambda i: (i, 0))
    once = lambda a: pl.BlockSpec(a.shape, lambda i: (0, 0), pipeline_mode=pl.Buffered(1))
    return pl.pallas_call(
        _moe_kernel,
        grid=(T // tm,),
        in_specs=[row(D), row(LANES), row(D), once(wg), once(wu), once(wd)],
        out_specs=row(D),
        out_shape=jax.ShapeDtypeStruct((T, D), jnp.float32),
        scratch_shapes=[pltpu.VMEM((tm, F), jnp.bfloat16)],
        compiler_params=_params("parallel"),
        name="moe",
    )(h2, comb, x1, wg, wu, wd)


def _layer(x, norm1_g, w_in, lam_re, lam_im, log_dt, ssm_b_re, ssm_b_im, ssm_c_re, ssm_c_im,
           ssm_d, w_glu, b_glu, q_norm_g, k_norm_g, w_proj_ssm, w_proj_attn, w_out, norm2_g,
           w_router_group, b_router_group, w_router_expert, b_router_expert, w_gate, w_up, w_down):
    B, S, D = x.shape
    bf = jnp.bfloat16
    x2 = x.reshape(B * S, D)
    proj = _in_proj(x2, norm1_g, w_in.astype(bf))
    q, k_bf, v_bf, kmean = _qk_prep(proj, q_norm_g, k_norm_g, B, S)
    att = _moba(q, k_bf, v_bf, kmean.reshape(B, S // MOBA_BLOCK, ATTN_WIDTH))
    toep, w_st, w_o_ssm, a_chunk = _ssm_weights(lam_re, lam_im, log_dt, ssm_b_re, ssm_b_im,
                                               ssm_c_re, ssm_c_im)
    y_raw = _ssm(proj[:, :SSM_WIDTH], toep, w_st, w_o_ssm, a_chunk, B, S)
    pad = jnp.zeros((D, LANES - N_EXPERTS - N_GROUPS), jnp.float32)
    w_r = jnp.concatenate([w_router_expert, w_router_group, pad], axis=1)
    b_r = jnp.concatenate([b_router_expert, b_router_group,
                           jnp.zeros((LANES - N_EXPERTS - N_GROUPS,), jnp.float32)]).reshape(1, LANES)
    x1, h2, comb = _merge(y_raw, proj, att.reshape(B * S, ATTN_WIDTH), x2, ssm_d,
                          w_glu.astype(bf), b_glu, w_proj_ssm.astype(bf), w_proj_attn.astype(bf),
                          w_out.astype(bf), norm2_g, w_r, b_r)
    F = N_EXPERTS * D_EXPERT
    wg = w_gate.transpose(1, 0, 2).reshape(D, F).astype(bf)
    wu = w_up.transpose(1, 0, 2).reshape(D, F).astype(bf)
    wd = w_down.reshape(F, D).astype(bf)
    out = _moe(h2, comb, x1, wg, wu, wd)
    return out.reshape(B, S, D)


def kernel(x, norm1_g, w_in, lam_re, lam_im, log_dt, ssm_b_re, ssm_b_im, ssm_c_re, ssm_c_im,
           ssm_d, w_glu, b_glu, q_norm_g, k_norm_g, w_proj_ssm, w_proj_attn, w_out, norm2_g,
           w_router_group, b_router_group, w_router_expert, b_router_expert, w_gate, w_up, w_down):
    args = (norm1_g, w_in, lam_re, lam_im, log_dt, ssm_b_re, ssm_b_im, ssm_c_re, ssm_c_im,
            ssm_d, w_glu, b_glu, q_norm_g, k_norm_g, w_proj_ssm, w_proj_attn, w_out, norm2_g,
            w_router_group, b_router_group, w_router_expert, b_router_expert, w_gate, w_up, w_down)
    for l in range(w_in.shape[0]):
        x = _layer(x, *(a[l] for a in args))
    return x
```

```python
import functools
import math

import jax
import jax.numpy as jnp
from jax import lax
from jax.experimental import pallas as pl
from jax.experimental.pallas import tpu as pltpu

D_MODEL = 1024
SSM_WIDTH = 512
SSM_GROUP = 16
SSM_GROUPS = SSM_WIDTH // SSM_GROUP
SSM_STATE = 64
N_HEADS = 8
HEAD_DIM = 64
ATTN_WIDTH = N_HEADS * HEAD_DIM
MOBA_BLOCK = 256
MOBA_TOPK = 3
ROPE_THETA = 500000.0
ROPE_DIM = HEAD_DIM // 4
N_GROUPS = 4
EXPERTS_PER_GROUP = 8
N_EXPERTS = N_GROUPS * EXPERTS_PER_GROUP
D_EXPERT = 128
EPS = 1e-6
IN_WIDTH = SSM_WIDTH + 3 * ATTN_WIDTH + 2 * D_MODEL

LANES = 128
SUBLANES = 8
VMEM_LIMIT = 56 * 1024 * 1024

SSM_CHUNK = 32
NEG_BIG = -1e30
MOBA_KV_CHUNK = 1024

_NT = (((1,), (1,)), ((), ()))


def _params(*sem):
    return pltpu.CompilerParams(dimension_semantics=sem, vmem_limit_bytes=VMEM_LIMIT)


def _in_proj_kernel(x_ref, g_ref, w_ref, o_ref):
    x = x_ref[...]
    ms = jnp.mean(x * x, axis=-1, keepdims=True)
    h = (x * lax.rsqrt(ms + EPS) * g_ref[...]).astype(jnp.bfloat16)
    n_out = o_ref.shape[1]
    step = 512
    for c in range(n_out // step):
        o_ref[:, c * step:(c + 1) * step] = jnp.dot(
            h, w_ref[:, c * step:(c + 1) * step], preferred_element_type=jnp.float32)


def _in_proj(x2, g, w_bf, tm=256):
    T, D = x2.shape
    N = w_bf.shape[1]
    return pl.pallas_call(
        _in_proj_kernel,
        grid=(T // tm,),
        in_specs=[pl.BlockSpec((tm, D), lambda i: (i, 0)),
                  pl.BlockSpec((1, D), lambda i: (0, 0)),
                  pl.BlockSpec((D, N), lambda i: (0, 0))],
        out_specs=pl.BlockSpec((tm, N), lambda i: (i, 0)),
        out_shape=jax.ShapeDtypeStruct((T, N), jnp.float32),
        compiler_params=_params("parallel"),
        name="in_proj",
    )(x2, g.reshape(1, D), w_bf)


def _head_norm_rope(x, g, ct, sa, sb):
    ts = x.shape[0]
    lane = lax.broadcasted_iota(jnp.int32, (ts, LANES), 1)
    lo = lane < HEAD_DIM
    outs = []
    for t in range(x.shape[1] // LANES):
        xt = x[:, t * LANES:(t + 1) * LANES]
        sq = xt * xt
        s_lo = jnp.sum(jnp.where(lo, sq, 0.0), axis=-1, keepdims=True)
        s_hi = jnp.sum(jnp.where(lo, 0.0, sq), axis=-1, keepdims=True)
        ms = jnp.where(lo, s_lo, s_hi) * (1.0 / HEAD_DIM)
        y = xt * lax.rsqrt(ms + EPS) * g
        y = y * ct + pltpu.roll(y, LANES - ROPE_DIM // 2, 1) * sa + pltpu.roll(y, ROPE_DIM // 2, 1) * sb
        outs.append(y)
    return jnp.concatenate(outs, axis=1)


def _qk_prep_kernel(q_ref, k_ref, v_ref, gq_ref, gk_ref, ct_ref, sa_ref, sb_ref,
                    qo_ref, ko_ref, vo_ref, km_ref):
    ct, sa, sb = ct_ref[...], sa_ref[...], sb_ref[...]
    qn = _head_norm_rope(q_ref[...], gq_ref[...], ct, sa, sb)
    kn = _head_norm_rope(k_ref[...], gk_ref[...], ct, sa, sb)
    qo_ref[0] = qn
    ko_ref[0] = kn.astype(jnp.bfloat16)
    vo_ref[0] = v_ref[...].T.astype(jnp.bfloat16)
    km_ref[0, 0] = jnp.mean(kn, axis=0, keepdims=True)


def _rope_tables(S):
    half = ROPE_DIM // 2
    inv_freq = ROPE_THETA ** (-jnp.arange(half, dtype=jnp.float32) * 2.0 / ROPE_DIM)
    ang = jnp.arange(S, dtype=jnp.float32)[:, None] * inv_freq[None, :]
    cos, sin = jnp.cos(ang), jnp.sin(ang)
    ones = jnp.ones((S, HEAD_DIM - ROPE_DIM), jnp.float32)
    zeros = jnp.zeros((S, HEAD_DIM - ROPE_DIM), jnp.float32)
    zh = jnp.zeros((S, half), jnp.float32)
    ct = jnp.concatenate([cos, cos, ones], axis=1)
    sa = jnp.concatenate([-sin, zh, zeros], axis=1)
    sb = jnp.concatenate([zh, sin, zeros], axis=1)
    tile = lambda t: jnp.concatenate([t, t], axis=1)
    return tile(ct), tile(sa), tile(sb)


def _qk_prep(proj, gq, gk, B, S):
    ts = MOBA_BLOCK
    nb = S // ts
    W = ATTN_WIDTH
    ct, sa, sb = _rope_tables(S)
    g128 = lambda g: jnp.concatenate([g, g]).reshape(1, LANES)
    row = lambda b, s: b * nb + s
    tab = pl.BlockSpec((ts, LANES), lambda b, s: (s, 0))
    out3 = pl.BlockSpec((1, ts, W), lambda b, s: (b, s, 0))
    return pl.pallas_call(
        _qk_prep_kernel,
        grid=(B, nb),
        in_specs=[pl.BlockSpec((ts, W), lambda b, s: (row(b, s), 1)),
                  pl.BlockSpec((ts, W), lambda b, s: (row(b, s), 2)),
                  pl.BlockSpec((ts, W), lambda b, s: (row(b, s), 3)),
                  pl.BlockSpec((1, LANES), lambda b, s: (0, 0)),
                  pl.BlockSpec((1, LANES), lambda b, s: (0, 0)),
                  tab, tab, tab],
        out_specs=[out3, out3, pl.BlockSpec((1, W, ts), lambda b, s: (b, 0, s)),
                   pl.BlockSpec((1, 1, 1, W), lambda b, s: (b, s, 0, 0))],
        out_shape=[jax.ShapeDtypeStruct((B, S, W), jnp.float32),
                   jax.ShapeDtypeStruct((B, S, W), jnp.bfloat16),
                   jax.ShapeDtypeStruct((B, W, S), jnp.bfloat16),
                   jax.ShapeDtypeStruct((B, nb, 1, W), jnp.float32)],
        compiler_params=_params("parallel", "parallel"),
        name="qk_prep",
    )(proj, proj, proj, g128(gq), g128(gk), ct, sa, sb)


def _moba_kernel(q_ref, k_ref, vt_ref, km_ref, oh_ref, o_ref, s_ref):
    i = pl.program_id(2)
    tq = q_ref.shape[1]
    nb = km_ref.shape[1]
    bf = jnp.bfloat16
    f32 = jnp.float32
    q = q_ref[0]
    km = km_ref[0]
    lane = lax.broadcasted_iota(jnp.int32, (tq, LANES), 1)
    blk = lax.broadcasted_iota(jnp.int32, (nb, tq), 0)
    past = blk < i
    scale = HEAD_DIM ** -0.5 * math.log2(math.e)

    q_aug = []
    for h in range(2):
        hmask = (lane < HEAD_DIM) if h == 0 else (lane >= HEAD_DIM)
        qh = jnp.where(hmask, q, 0.0)
        s = lax.dot_general(km, qh, _NT, precision=lax.Precision.HIGHEST,
                            preferred_element_type=f32)
        s = jnp.where(past, s, -jnp.inf)
        sel = jnp.zeros((nb, tq), jnp.bool_)
        for _ in range(MOBA_TOPK):
            mx = jnp.max(s, axis=0, keepdims=True)
            first = jnp.min(jnp.where(s == mx, blk, nb), axis=0, keepdims=True)
            pick = blk == first
            sel = jnp.logical_or(sel, jnp.logical_and(pick, past))
            s = jnp.where(pick, -jnp.inf, s)
        bias_t = jnp.where(sel, 0.0, NEG_BIG)
        bias_t = jnp.concatenate([bias_t, jnp.zeros((LANES - nb, tq), f32)], axis=0)
        bias = bias_t.T.astype(bf)
        q_aug.append(jnp.concatenate([(qh * scale).astype(bf), bias], axis=1))

    ck = MOBA_KV_CHUNK
    n_chunks = lax.shift_right_logical(i + (ck // tq - 1), int(math.log2(ck // tq)))
    diag_row = s_ref.shape[1] - tq

    off_d = pl.multiple_of(i * tq, tq)
    kd = k_ref[0, pl.ds(off_d, tq), :]
    key_i = lax.broadcasted_iota(jnp.int32, (tq, tq), 0)
    qry_i = lax.broadcasted_iota(jnp.int32, (tq, tq), 1)
    causal = key_i <= qry_i
    m0 = []
    for h in range(2):
        s = lax.dot_general(kd, q_aug[h][:, :LANES], _NT, preferred_element_type=f32)
        s = jnp.where(causal, s, -jnp.inf)
        s_ref[h, diag_row:, :] = s
        m0.append(jnp.max(s, axis=0, keepdims=True))

    def logits_body(c, ms):
        off = pl.multiple_of(c * ck, ck)
        k_aug = jnp.concatenate([k_ref[0, pl.ds(off, ck), :], oh_ref[pl.ds(off, ck), :]], axis=1)
        out = []
        for h in range(2):
            s = lax.dot_general(k_aug, q_aug[h], _NT, preferred_element_type=f32)
            s_ref[h, pl.ds(off, ck), :] = s
            out.append(jnp.maximum(ms[h], jnp.max(s, axis=0, keepdims=True)))
        return tuple(out)

    ms = lax.fori_loop(0, n_chunks, logits_body, tuple(m0))

    def pv_body(c, carry):
        off = pl.multiple_of(c * ck, ck)
        out = []
        for h in range(2):
            p = jnp.exp2(s_ref[h, pl.ds(off, ck), :] - ms[h])
            l = carry[2 * h] + jnp.sum(p, axis=0, keepdims=True)
            vt = vt_ref[0, h * HEAD_DIM:(h + 1) * HEAD_DIM, pl.ds(off, ck)]
            acc = carry[2 * h + 1] + jnp.dot(vt, p.astype(bf), preferred_element_type=f32)
            out += [l, acc]
        return tuple(out)

    init = []
    for h in range(2):
        p = jnp.exp2(s_ref[h, diag_row:, :] - ms[h])
        vt = vt_ref[0, h * HEAD_DIM:(h + 1) * HEAD_DIM, pl.ds(off_d, tq)]
        init += [jnp.sum(p, axis=0, keepdims=True),
                 jnp.dot(vt, p.astype(bf), preferred_element_type=f32)]
    l0, a0, l1, a1 = lax.fori_loop(0, n_chunks, pv_body, tuple(init))
    o_t = jnp.concatenate([a0 / l0, a1 / l1], axis=0)
    o_ref[0] = o_t.T.astype(o_ref.dtype)


def _moba(q, k_bf, vt_bf, kmean):
    B, S, W = q.shape
    tq = MOBA_BLOCK
    nb = S // tq
    onehot = (jnp.arange(S)[:, None] // tq == jnp.arange(LANES)[None, :]).astype(jnp.bfloat16)
    qo = pl.BlockSpec((1, tq, LANES), lambda b, hp, i: (b, i, hp))
    return pl.pallas_call(
        _moba_kernel,
        grid=(B, W // LANES, nb),
        in_specs=[qo,
                  pl.BlockSpec((1, S, LANES), lambda b, hp, i: (b, 0, hp)),
                  pl.BlockSpec((1, LANES, S), lambda b, hp, i: (b, hp, 0)),
                  pl.BlockSpec((1, nb, LANES), lambda b, hp, i: (b, 0, hp)),
                  pl.BlockSpec((S, LANES), lambda b, hp, i: (0, 0))],
        out_specs=qo,
        out_shape=jax.ShapeDtypeStruct((B, S, W), jnp.bfloat16),
        scratch_shapes=[pltpu.VMEM((2, S + tq, tq), jnp.float32)],
        compiler_params=_params("parallel", "parallel", "arbitrary"),
        name="moba",
    )(q, k_bf, vt_bf, kmean, onehot)


def _ssm_weights(lam_re, lam_im, log_dt, b_re, b_im, c_re, c_im):
    L = SSM_CHUNK
    f32 = jnp.float32
    dt = jnp.exp(log_dt.astype(f32))[:, None]
    lr = jnp.minimum(lam_re.astype(f32), -1e-4)
    li = lam_im.astype(f32)
    mag = jnp.exp(lr * dt)
    ar = mag * jnp.cos(li * dt)
    ai = mag * jnp.sin(li * dt)
    den = lr * lr + li * li
    nr, ni = ar - 1.0, ai
    cr = (nr * lr + ni * li) / den
    ci = (ni * lr - nr * li) / den
    brf, bif = b_re.astype(f32), b_im.astype(f32)
    bbr = cr[..., None] * brf - ci[..., None] * bif
    bbi = cr[..., None] * bif + ci[..., None] * brf
    tau = jnp.arange(L + 1, dtype=f32)[:, None, None]
    pmag = jnp.exp(tau * (lr * dt)[None])
    pr = pmag * jnp.cos(tau * (li * dt)[None])
    pi = pmag * jnp.sin(tau * (li * dt)[None])
    cre, cim = c_re.astype(f32), c_im.astype(f32)
    hi = lax.Precision.HIGHEST
    car = cre[None] * pr[:, :, None, :] - cim[None] * pi[:, :, None, :]
    cai = cre[None] * pi[:, :, None, :] + cim[None] * pr[:, :, None, :]
    kern = (jnp.einsum('tgon,gni->tgoi', car[:L], bbr, precision=hi)
            - jnp.einsum('tgon,gni->tgoi', cai[:L], bbi, precision=hi))
    C = SSM_GROUP
    P = 2 * L * C
    kt = jnp.concatenate([kern, jnp.zeros_like(kern)], axis=0).astype(jnp.bfloat16)
    kt = kt.transpose(1, 3, 0, 2).reshape(SSM_GROUPS, C, P)
    toep = jnp.tile(kt, (1, 1, L))[..., :L * (P - C)].reshape(SSM_GROUPS, C, L, P - C)[..., :L * C]
    toep = toep.transpose(0, 2, 1, 3).reshape(SSM_GROUPS, L * C, L * C)
    rev_r, rev_i = pr[L - 1::-1][:L], pi[L - 1::-1][:L]
    st_r = rev_r[:, :, :, None] * bbr[None] - rev_i[:, :, :, None] * bbi[None]
    st_i = rev_r[:, :, :, None] * bbi[None] + rev_i[:, :, :, None] * bbr[None]
    w_st = jnp.concatenate([st_r, st_i], axis=2)
    w_st = w_st.transpose(1, 0, 3, 2).reshape(SSM_GROUPS, L * SSM_GROUP, 2 * SSM_STATE)
    w_out = jnp.concatenate([car[1:], -cai[1:]], axis=3)
    w_out = w_out.transpose(1, 3, 0, 2).reshape(SSM_GROUPS, 2 * SSM_STATE, L * SSM_GROUP)
    a_chunk = jnp.concatenate([pr[L], pi[L]], axis=1)
    return toep, w_st, w_out, a_chunk


def _ssm_local_kernel(u_ref, w_ref, o_ref):
    o_ref[0] = jnp.dot(u_ref[0], w_ref[0], preferred_element_type=jnp.float32)


def _ssm_scan_kernel(s_ref, p_ref, q_ref, o_ref):
    n_chunks = s_ref.shape[1]
    pm, qm = p_ref[...], q_ref[...]

    def body(c, h):
        o_ref[0, c] = h
        return h * pm + pltpu.roll(h, SSM_STATE, 1) * qm + s_ref[0, c]

    lax.fori_loop(0, n_chunks, body, jnp.zeros(pm.shape, jnp.float32))


def _ssm_out_kernel(u_ref, h_ref, t_ref, w_ref, o_ref):
    y = jnp.dot(u_ref[0], t_ref[0], preferred_element_type=jnp.float32)
    y += jnp.dot(h_ref[0].astype(jnp.bfloat16), w_ref[0], preferred_element_type=jnp.float32)
    o_ref[0] = y


def _ssm(u, toep, w_st, w_out, a_chunk, B, S):
    L, G, C16 = SSM_CHUNK, SSM_GROUPS, SSM_GROUP
    nc = S // L
    R = B * nc
    K = L * C16
    N2 = 2 * SSM_STATE
    bf = jnp.bfloat16
    ug = u.reshape(R, L, G, C16).transpose(2, 0, 1, 3).reshape(G, R, K).astype(bf)
    s_loc = pl.pallas_call(
        _ssm_local_kernel,
        grid=(G,),
        in_specs=[pl.BlockSpec((1, R, K), lambda g: (g, 0, 0)),
                  pl.BlockSpec((1, K, N2), lambda g: (g, 0, 0))],
        out_specs=pl.BlockSpec((1, R, N2), lambda g: (g, 0, 0)),
        out_shape=jax.ShapeDtypeStruct((G, R, N2), jnp.float32),
        compiler_params=_params("parallel"),
        name="ssm_local",
    )(ug, w_st.astype(bf))
    s_bc = s_loc.reshape(G, B, nc, N2).transpose(1, 2, 0, 3)
    ar, ai = a_chunk[:, :SSM_STATE], a_chunk[:, SSM_STATE:]
    pm = jnp.concatenate([ar, ar], axis=1)
    qm = jnp.concatenate([-ai, ai], axis=1)
    h_in = pl.pallas_call(
        _ssm_scan_kernel,
        grid=(B,),
        in_specs=[pl.BlockSpec((1, nc, G, N2), lambda b: (b, 0, 0, 0)),
                  pl.BlockSpec((G, N2), lambda b: (0, 0)),
                  pl.BlockSpec((G, N2), lambda b: (0, 0))],
        out_specs=pl.BlockSpec((1, nc, G, N2), lambda b: (b, 0, 0, 0)),
        out_shape=jax.ShapeDtypeStruct((B, nc, G, N2), jnp.float32),
        compiler_params=_params("parallel"),
        name="ssm_scan",
    )(s_bc, pm, qm)
    h_g = h_in.transpose(2, 0, 1, 3).reshape(G, R, N2)
    y_g = pl.pallas_call(
        _ssm_out_kernel,
        grid=(G,),
        in_specs=[pl.BlockSpec((1, R, K), lambda g: (g, 0, 0)),
                  pl.BlockSpec((1, R, N2), lambda g: (g, 0, 0)),
                  pl.BlockSpec((1, K, K), lambda g: (g, 0, 0)),
                  pl.BlockSpec((1, N2, K), lambda g: (g, 0, 0))],
        out_specs=pl.BlockSpec((1, R, K), lambda g: (g, 0, 0)),
        out_shape=jax.ShapeDtypeStruct((G, R, K), jnp.float32),
        compiler_params=_params("parallel"),
        name="ssm_out",
    )(ug, h_g, toep.astype(bf), w_out.astype(bf))
    return y_g.reshape(G, R, L, C16).transpose(1, 2, 0, 3).reshape(B * S, SSM_WIDTH)


def _route(logits):
    tm = logits.shape[0]
    lane = lax.broadcasted_iota(jnp.int32, (tm, LANES), 1)
    is_g = jnp.logical_and(lane >= N_EXPERTS, lane < N_EXPERTS + N_GROUPS)
    gl = jnp.where(is_g, logits, -jnp.inf)
    gmax = jnp.max(gl, axis=-1, keepdims=True)
    ge = jnp.exp(gl - gmax)
    gp = ge / jnp.sum(ge, axis=-1, keepdims=True)
    g_p = jnp.max(gp, axis=-1, keepdims=True)
    g_lane = jnp.min(jnp.where(jnp.logical_and(is_g, gp == g_p), lane, LANES), axis=-1, keepdims=True)
    g_idx = g_lane - N_EXPERTS
    grp_of_lane = lax.shift_right_logical(lane, int(math.log2(EXPERTS_PER_GROUP)))
    in_grp = jnp.logical_and(lane < N_EXPERTS, grp_of_lane == g_idx)
    el = jnp.where(in_grp, logits, -jnp.inf)
    v1 = jnp.max(el, axis=-1, keepdims=True)
    i1 = jnp.min(jnp.where(el == v1, lane, LANES), axis=-1, keepdims=True)
    el2 = jnp.where(lane == i1, -jnp.inf, el)
    v2 = jnp.max(el2, axis=-1, keepdims=True)
    i2 = jnp.min(jnp.where(el2 == v2, lane, LANES), axis=-1, keepdims=True)
    e2 = jnp.exp(v2 - v1)
    den = 1.0 + e2
    w1 = (1.0 / den) * g_p
    w2 = (e2 / den) * g_p
    return jnp.where(lane == i1, w1, jnp.where(lane == i2, w2, 0.0))


def _merge_kernel(y_ref, u_ref, att_ref, gs_ref, ga_ref, x_ref, d_ref, wglu_ref, bglu_ref,
                  wps_ref, wpa_ref, wo_ref, g2_ref, wr_ref, br_ref,
                  x1_ref, h2_ref, comb_ref):
    bf = jnp.bfloat16
    f32 = jnp.float32
    y = y_ref[...] + d_ref[...] * u_ref[...]
    z = jax.nn.gelu(y)
    gate = jnp.dot(z.astype(bf), wglu_ref[...], preferred_element_type=f32) + bglu_ref[...]
    s5 = z * jax.nn.sigmoid(gate)
    y_ssm = jnp.dot(s5.astype(bf), wps_ref[...], preferred_element_type=f32)
    y_att = jnp.dot(att_ref[...], wpa_ref[...], preferred_element_type=f32)
    merged = jax.nn.sigmoid(gs_ref[...]) * y_ssm + jax.nn.sigmoid(ga_ref[...]) * y_att
    x1 = x_ref[...] + jnp.dot(merged.astype(bf), wo_ref[...], preferred_element_type=f32)
    x1_ref[...] = x1
    ms = jnp.mean(x1 * x1, axis=-1, keepdims=True)
    h2 = x1 * lax.rsqrt(ms + EPS) * g2_ref[...]
    h2_ref[...] = h2.astype(bf)
    logits = jnp.dot(h2, wr_ref[...], precision=lax.Precision.HIGHEST,
                     preferred_element_type=f32) + br_ref[...]
    comb_ref[...] = _route(logits)


def _merge(y_raw, proj, att, x2, d, w_glu, b_glu, w_ps, w_pa, w_o, g2, w_r, b_r, tm=256):
    T, D = x2.shape
    W = SSM_WIDTH
    row = lambda shape, col: pl.BlockSpec(shape, lambda i: (i, col))
    full = lambda a: pl.BlockSpec(a.shape, lambda i: (0,) * a.ndim)
    consts = [d.reshape(1, W), w_glu, b_glu.reshape(1, W), w_ps, w_pa, w_o, g2.reshape(1, D), w_r, b_r]
    return pl.pallas_call(
        _merge_kernel,
        grid=(T // tm,),
        in_specs=[row((tm, W), 0), row((tm, W), 0), row((tm, W), 0),
                  row((tm, D), 2), row((tm, D), 3), row((tm, D), 0)] + [full(c) for c in consts],
        out_specs=[row((tm, D), 0), row((tm, D), 0), row((tm, LANES), 0)],
        out_shape=[jax.ShapeDtypeStruct((T, D), jnp.float32),
                   jax.ShapeDtypeStruct((T, D), jnp.bfloat16),
                   jax.ShapeDtypeStruct((T, LANES), jnp.float32)],
        compiler_params=_params("parallel"),
        name="merge",
    )(y_raw, proj, att, proj, proj, x2, *consts)


def _moe_kernel(h_ref, comb_ref, x_ref, wg_ref, wu_ref, wd_ref, o_ref, act_ref):
    f32 = jnp.float32
    h = h_ref[...]
    comb = comb_ref[...]
    step = 4 * D_EXPERT
    for c in range(N_EXPERTS * D_EXPERT // step):
        hg = jnp.dot(h, wg_ref[:, c * step:(c + 1) * step], preferred_element_type=f32)
        hu = jnp.dot(h, wu_ref[:, c * step:(c + 1) * step], preferred_element_type=f32)
        a = jax.nn.silu(hg) * hu
        for j in range(step // D_EXPERT):
            e = c * (step // D_EXPERT) + j
            act_ref[:, e * D_EXPERT:(e + 1) * D_EXPERT] = (
                a[:, j * D_EXPERT:(j + 1) * D_EXPERT] * comb[:, e:e + 1]).astype(act_ref.dtype)
    o_ref[...] = x_ref[...] + jnp.dot(act_ref[...], wd_ref[...], preferred_element_type=f32)


def _moe(h2, comb, x1, wg, wu, wd, tm=512):
    T, D = x1.shape
    F = N_EXPERTS * D_EXPERT
    row = lambda w: pl.BlockSpec((tm, w), lambda i: (i, 0))
    once = lambda a: pl.BlockSpec(a.shape, lambda i: (0, 0), pipeline_mode=pl.Buffered(1))
    return pl.pallas_call(
        _moe_kernel,
        grid=(T // tm,),
        in_specs=[row(D), row(LANES), row(D), once(wg), once(wu), once(wd)],
        out_specs=row(D),
        out_shape=jax.ShapeDtypeStruct((T, D), jnp.float32),
        scratch_shapes=[pltpu.VMEM((tm, F), jnp.bfloat16)],
        compiler_params=_params("parallel"),
        name="moe",
    )(h2, comb, x1, wg, wu, wd)


def _layer(x, norm1_g, w_in, lam_re, lam_im, log_dt, ssm_b_re, ssm_b_im, ssm_c_re, ssm_c_im,
           ssm_d, w_glu, b_glu, q_norm_g, k_norm_g, w_proj_ssm, w_proj_attn, w_out, norm2_g,
           w_router_group, b_router_group, w_router_expert, b_router_expert, w_gate, w_up, w_down):
    B, S, D = x.shape
    bf = jnp.bfloat16
    x2 = x.reshape(B * S, D)
    proj = _in_proj(x2, norm1_g, w_in.astype(bf))
    q, k_bf, v_bf, kmean = _qk_prep(proj, q_norm_g, k_norm_g, B, S)
    att = _moba(q, k_bf, v_bf, kmean.reshape(B, S // MOBA_BLOCK, ATTN_WIDTH))
    toep, w_st, w_o_ssm, a_chunk = _ssm_weights(lam_re, lam_im, log_dt, ssm_b_re, ssm_b_im,
                                               ssm_c_re, ssm_c_im)
    y_raw = _ssm(proj[:, :SSM_WIDTH], toep, w_st, w_o_ssm, a_chunk, B, S)
    pad = jnp.zeros((D, LANES - N_EXPERTS - N_GROUPS), jnp.float32)
    w_r = jnp.concatenate([w_router_expert, w_router_group, pad], axis=1)
    b_r = jnp.concatenate([b_router_expert, b_router_group,
                           jnp.zeros((LANES - N_EXPERTS - N_GROUPS,), jnp.float32)]).reshape(1, LANES)
    x1, h2, comb = _merge(y_raw, proj, att.reshape(B * S, ATTN_WIDTH), x2, ssm_d,
                          w_glu.astype(bf), b_glu, w_proj_ssm.astype(bf), w_proj_attn.astype(bf),
                          w_out.astype(bf), norm2_g, w_r, b_r)
    F = N_EXPERTS * D_EXPERT
    wg = w_gate.transpose(1, 0, 2).reshape(D, F).astype(bf)
    wu = w_up.transpose(1, 0, 2).reshape(D, F).astype(bf)
    wd = w_down.reshape(F, D).astype(bf)
    out = _moe(h2, comb, x1, wg, wu, wd)
    return out.reshape(B, S, D)


def kernel(x, norm1_g, w_in, lam_re, lam_im, log_dt, ssm_b_re, ssm_b_im, ssm_c_re, ssm_c_im,
           ssm_d, w_glu, b_glu, q_norm_g, k_norm_g, w_proj_ssm, w_proj_attn, w_out, norm2_g,
           w_router_group, b_router_group, w_router_expert, b_router_expert, w_gate, w_up, w_down):
    args = (norm1_g, w_in, lam_re, lam_im, log_dt, ssm_b_re, ssm_b_im, ssm_c_re, ssm_c_im,
            ssm_d, w_glu, b_glu, q_norm_g, k_norm_g, w_proj_ssm, w_proj_attn, w_out, norm2_g,
            w_router_group, b_router_group, w_router_expert, b_router_expert, w_gate, w_up, w_down)
    for l in range(w_in.shape[0]):
        x = _layer(x, *(a[l] for a in args))
    return x
```

```python
import functools
import math

import jax
import jax.numpy as jnp
from jax import lax
from jax.experimental import pallas as pl
from jax.experimental.pallas import tpu as pltpu

D_MODEL = 1024
SSM_WIDTH = 512
SSM_GROUP = 16
SSM_GROUPS = SSM_WIDTH // SSM_GROUP
SSM_STATE = 64
N_HEADS = 8
HEAD_DIM = 64
ATTN_WIDTH = N_HEADS * HEAD_DIM
MOBA_BLOCK = 256
MOBA_TOPK = 3
ROPE_THETA = 500000.0
ROPE_DIM = HEAD_DIM // 4
N_GROUPS = 4
EXPERTS_PER_GROUP = 8
N_EXPERTS = N_GROUPS * EXPERTS_PER_GROUP
D_EXPERT = 128
EPS = 1e-6
IN_WIDTH = SSM_WIDTH + 3 * ATTN_WIDTH + 2 * D_MODEL

LANES = 128
SUBLANES = 8
VMEM_LIMIT = 56 * 1024 * 1024

SSM_CHUNK = 32
NEG_BIG = -1e30
MOBA_KV_CHUNK = 1024

_NT = (((1,), (1,)), ((), ()))


def _params(*sem):
    return pltpu.CompilerParams(dimension_semantics=sem, vmem_limit_bytes=VMEM_LIMIT)


def _dot_split(a, b, dims):
    bf, f32 = jnp.bfloat16, jnp.float32
    a_hi, b_hi = a.astype(bf), b.astype(bf)
    a_lo = (a - a_hi.astype(f32)).astype(bf)
    b_lo = (b - b_hi.astype(f32)).astype(bf)
    dot = lambda x, y: lax.dot_general(x, y, dims, preferred_element_type=f32)
    return dot(a_hi, b_hi) + (dot(a_hi, b_lo) + dot(a_lo, b_hi))


def _in_proj_kernel(x_ref, g_ref, w_ref, o_ref):
    x = x_ref[...]
    ms = jnp.mean(x * x, axis=-1, keepdims=True)
    h = (x * lax.rsqrt(ms + EPS) * g_ref[...]).astype(jnp.bfloat16)
    n_out = o_ref.shape[1]
    step = 512
    for c in range(n_out // step):
        o_ref[:, c * step:(c + 1) * step] = jnp.dot(
            h, w_ref[:, c * step:(c + 1) * step], preferred_element_type=jnp.float32)


def _in_proj(x2, g, w_bf, tm=256):
    T, D = x2.shape
    N = w_bf.shape[1]
    return pl.pallas_call(
        _in_proj_kernel,
        grid=(T // tm,),
        in_specs=[pl.BlockSpec((tm, D), lambda i: (i, 0)),
                  pl.BlockSpec((1, D), lambda i: (0, 0)),
                  pl.BlockSpec((D, N), lambda i: (0, 0))],
        out_specs=pl.BlockSpec((tm, N), lambda i: (i, 0)),
        out_shape=jax.ShapeDtypeStruct((T, N), jnp.float32),
        compiler_params=_params("parallel"),
        name="in_proj",
    )(x2, g.reshape(1, D), w_bf)


def _head_norm_rope(x, g, ct, sa, sb):
    ts = x.shape[0]
    lane = lax.broadcasted_iota(jnp.int32, (ts, LANES), 1)
    lo = lane < HEAD_DIM
    outs = []
    for t in range(x.shape[1] // LANES):
        xt = x[:, t * LANES:(t + 1) * LANES]
        sq = xt * xt
        s_lo = jnp.sum(jnp.where(lo, sq, 0.0), axis=-1, keepdims=True)
        s_hi = jnp.sum(jnp.where(lo, 0.0, sq), axis=-1, keepdims=True)
        ms = jnp.where(lo, s_lo, s_hi) * (1.0 / HEAD_DIM)
        y = xt * lax.rsqrt(ms + EPS) * g
        y = y * ct + pltpu.roll(y, LANES - ROPE_DIM // 2, 1) * sa + pltpu.roll(y, ROPE_DIM // 2, 1) * sb
        outs.append(y)
    return jnp.concatenate(outs, axis=1)


def _qk_prep_kernel(q_ref, k_ref, v_ref, gq_ref, gk_ref, ct_ref, sa_ref, sb_ref,
                    qo_ref, ko_ref, vo_ref, km_ref):
    ct, sa, sb = ct_ref[...], sa_ref[...], sb_ref[...]
    qn = _head_norm_rope(q_ref[...], gq_ref[...], ct, sa, sb)
    kn = _head_norm_rope(k_ref[...], gk_ref[...], ct, sa, sb)
    qo_ref[0] = qn
    ko_ref[0] = kn.astype(jnp.bfloat16)
    vo_ref[0] = v_ref[...].T.astype(jnp.bfloat16)
    km_ref[0, 0] = jnp.mean(kn, axis=0, keepdims=True)


def _rope_tables(S):
    half = ROPE_DIM // 2
    inv_freq = ROPE_THETA ** (-jnp.arange(half, dtype=jnp.float32) * 2.0 / ROPE_DIM)
    ang = jnp.arange(S, dtype=jnp.float32)[:, None] * inv_freq[None, :]
    cos, sin = jnp.cos(ang), jnp.sin(ang)
    ones = jnp.ones((S, HEAD_DIM - ROPE_DIM), jnp.float32)
    zeros = jnp.zeros((S, HEAD_DIM - ROPE_DIM), jnp.float32)
    zh = jnp.zeros((S, half), jnp.float32)
    ct = jnp.concatenate([cos, cos, ones], axis=1)
    sa = jnp.concatenate([-sin, zh, zeros], axis=1)
    sb = jnp.concatenate([zh, sin, zeros], axis=1)
    tile = lambda t: jnp.concatenate([t, t], axis=1)
    return tile(ct), tile(sa), tile(sb)


def _qk_prep(proj, gq, gk, B, S):
    ts = MOBA_BLOCK
    nb = S // ts
    W = ATTN_WIDTH
    ct, sa, sb = _rope_tables(S)
    g128 = lambda g: jnp.concatenate([g, g]).reshape(1, LANES)
    row = lambda b, s: b * nb + s
    tab = pl.BlockSpec((ts, LANES), lambda b, s: (s, 0))
    out3 = pl.BlockSpec((1, ts, W), lambda b, s: (b, s, 0))
    return pl.pallas_call(
        _qk_prep_kernel,
        grid=(B, nb),
        in_specs=[pl.BlockSpec((ts, W), lambda b, s: (row(b, s), 1)),
                  pl.BlockSpec((ts, W), lambda b, s: (row(b, s), 2)),
                  pl.BlockSpec((ts, W), lambda b, s: (row(b, s), 3)),
                  pl.BlockSpec((1, LANES), lambda b, s: (0, 0)),
                  pl.BlockSpec((1, LANES), lambda b, s: (0, 0)),
                  tab, tab, tab],
        out_specs=[out3, out3, pl.BlockSpec((1, W, ts), lambda b, s: (b, 0, s)),
                   pl.BlockSpec((1, 1, 1, W), lambda b, s: (b, s, 0, 0))],
        out_shape=[jax.ShapeDtypeStruct((B, S, W), jnp.float32),
                   jax.ShapeDtypeStruct((B, S, W), jnp.bfloat16),
                   jax.ShapeDtypeStruct((B, W, S), jnp.bfloat16),
                   jax.ShapeDtypeStruct((B, nb, 1, W), jnp.float32)],
        compiler_params=_params("parallel", "parallel"),
        name="qk_prep",
    )(proj, proj, proj, g128(gq), g128(gk), ct, sa, sb)


def _moba_kernel(q_ref, k_ref, vt_ref, km_ref, oh_ref, o_ref, s_ref):
    i = pl.program_id(2)
    tq = q_ref.shape[1]
    nb = km_ref.shape[1]
    bf = jnp.bfloat16
    f32 = jnp.float32
    q = q_ref[0]
    km = km_ref[0]
    lane = lax.broadcasted_iota(jnp.int32, (tq, LANES), 1)
    blk = lax.broadcasted_iota(jnp.int32, (nb, tq), 0)
    past = blk < i
    scale = HEAD_DIM ** -0.5 * math.log2(math.e)

    q_aug = []
    for h in range(2):
        hmask = (lane < HEAD_DIM) if h == 0 else (lane >= HEAD_DIM)
        qh = jnp.where(hmask, q, 0.0)
        s = _dot_split(km, qh, _NT)
        s = jnp.where(past, s, -jnp.inf)
        sel = jnp.zeros((nb, tq), jnp.bool_)
        for _ in range(MOBA_TOPK):
            mx = jnp.max(s, axis=0, keepdims=True)
            first = jnp.min(jnp.where(s == mx, blk, nb), axis=0, keepdims=True)
            pick = blk == first
            sel = jnp.logical_or(sel, jnp.logical_and(pick, past))
            s = jnp.where(pick, -jnp.inf, s)
        bias_t = jnp.where(sel, 0.0, NEG_BIG)
        bias_t = jnp.concatenate([bias_t, jnp.zeros((LANES - nb, tq), f32)], axis=0)
        bias = bias_t.T.astype(bf)
        q_aug.append(jnp.concatenate([(qh * scale).astype(bf), bias], axis=1))

    ck = MOBA_KV_CHUNK
    n_chunks = lax.shift_right_logical(i + (ck // tq - 1), int(math.log2(ck // tq)))
    n_chunks = jnp.maximum(n_chunks, 1)

    def logits(c, slot):
        off = pl.multiple_of(c * ck, ck)
        k_aug = jnp.concatenate([k_ref[0, pl.ds(off, ck), :], oh_ref[pl.ds(off, ck), :]], axis=1)
        cm = []
        for h in range(2):
            s = lax.dot_general(k_aug, q_aug[h], _NT, preferred_element_type=f32)
            s_ref[slot, h] = s
            cm.append(jnp.max(s, axis=0, keepdims=True))
        return cm

    def weigh(s, vt, cm, m, l, acc):
        m_new = cm if m is None else jnp.maximum(m, cm)
        p = jnp.exp2(s - m_new)
        psum = jnp.sum(p, axis=0, keepdims=True)
        pv = jnp.dot(vt, p.astype(bf), preferred_element_type=f32)
        if m is None:
            return [m_new, psum, pv]
        alpha = jnp.exp2(m - m_new)
        return [m_new, alpha * l + psum, alpha * acc + pv]

    def weigh_chunk(c, slot, cm, state):
        off = pl.multiple_of(c * ck, ck)
        out = []
        for h in range(2):
            vt = vt_ref[0, h * HEAD_DIM:(h + 1) * HEAD_DIM, pl.ds(off, ck)]
            out += weigh(s_ref[slot, h], vt, cm[h], *state[3 * h:3 * h + 3])
        return out

    off_d = pl.multiple_of(i * tq, tq)
    kd = k_ref[0, pl.ds(off_d, tq), :]
    key_i = lax.broadcasted_iota(jnp.int32, (tq, tq), 0)
    qry_i = lax.broadcasted_iota(jnp.int32, (tq, tq), 1)
    causal = key_i <= qry_i
    state = []
    for h in range(2):
        s = lax.dot_general(kd, q_aug[h][:, :LANES], _NT, preferred_element_type=f32)
        s = jnp.where(causal, s, -jnp.inf)
        vt = vt_ref[0, h * HEAD_DIM:(h + 1) * HEAD_DIM, pl.ds(off_d, tq)]
        state += weigh(s, vt, jnp.max(s, axis=0, keepdims=True), None, None, None)

    cm0 = logits(0, 0)

    def body(c, carry):
        cm, state = carry[:2], carry[2:]
        state = weigh_chunk(c, c & 1, cm, state)
        cm_next = logits(c + 1, (c + 1) & 1)
        return tuple(cm_next) + tuple(state)

    carry = lax.fori_loop(0, n_chunks - 1, body, tuple(cm0) + tuple(state))
    last = n_chunks - 1
    _, l0, a0, _, l1, a1 = weigh_chunk(last, last & 1, carry[:2], carry[2:])
    o_t = jnp.concatenate([a0 / l0, a1 / l1], axis=0)
    o_ref[0] = o_t.T.astype(o_ref.dtype)


def _moba(q, k_bf, vt_bf, kmean):
    B, S, W = q.shape
    tq = MOBA_BLOCK
    nb = S // tq
    onehot = (jnp.arange(S)[:, None] // tq == jnp.arange(LANES)[None, :]).astype(jnp.bfloat16)
    qo = pl.BlockSpec((1, tq, LANES), lambda b, hp, i: (b, i, hp))
    return pl.pallas_call(
        _moba_kernel,
        grid=(B, W // LANES, nb),
        in_specs=[qo,
                  pl.BlockSpec((1, S, LANES), lambda b, hp, i: (b, 0, hp)),
                  pl.BlockSpec((1, LANES, S), lambda b, hp, i: (b, hp, 0)),
                  pl.BlockSpec((1, nb, LANES), lambda b, hp, i: (b, 0, hp)),
                  pl.BlockSpec((S, LANES), lambda b, hp, i: (0, 0))],
        out_specs=qo,
        out_shape=jax.ShapeDtypeStruct((B, S, W), jnp.bfloat16),
        scratch_shapes=[pltpu.VMEM((2, 2, MOBA_KV_CHUNK, tq), jnp.float32)],
        compiler_params=_params("parallel", "parallel", "arbitrary"),
        name="moba",
    )(q, k_bf, vt_bf, kmean, onehot)


def _ssm_weights(lam_re, lam_im, log_dt, b_re, b_im, c_re, c_im):
    L = SSM_CHUNK
    f32 = jnp.float32
    dt = jnp.exp(log_dt.astype(f32))[:, None]
    lr = jnp.minimum(lam_re.astype(f32), -1e-4)
    li = lam_im.astype(f32)
    mag = jnp.exp(lr * dt)
    ar = mag * jnp.cos(li * dt)
    ai = mag * jnp.sin(li * dt)
    den = lr * lr + li * li
    nr, ni = ar - 1.0, ai
    cr = (nr * lr + ni * li) / den
    ci = (ni * lr - nr * li) / den
    brf, bif = b_re.astype(f32), b_im.astype(f32)
    bbr = cr[..., None] * brf - ci[..., None] * bif
    bbi = cr[..., None] * bif + ci[..., None] * brf
    tau = jnp.arange(L + 1, dtype=f32)[:, None, None]
    pmag = jnp.exp(tau * (lr * dt)[None])
    pr = pmag * jnp.cos(tau * (li * dt)[None])
    pi = pmag * jnp.sin(tau * (li * dt)[None])
    cre, cim = c_re.astype(f32), c_im.astype(f32)
    hi = lax.Precision.HIGHEST
    car = cre[None] * pr[:, :, None, :] - cim[None] * pi[:, :, None, :]
    cai = cre[None] * pi[:, :, None, :] + cim[None] * pr[:, :, None, :]
    kern = (jnp.einsum('tgon,gni->tgoi', car[:L], bbr, precision=hi)
            - jnp.einsum('tgon,gni->tgoi', cai[:L], bbi, precision=hi))
    C = SSM_GROUP
    P = 2 * L * C
    kt = jnp.concatenate([kern, jnp.zeros_like(kern)], axis=0).astype(jnp.bfloat16)
    kt = kt.transpose(1, 3, 0, 2).reshape(SSM_GROUPS, C, P)
    toep = jnp.tile(kt, (1, 1, L))[..., :L * (P - C)].reshape(SSM_GROUPS, C, L, P - C)[..., :L * C]
    toep = toep.transpose(0, 2, 1, 3).reshape(SSM_GROUPS, L * C, L * C)
    rev_r, rev_i = pr[L - 1::-1][:L], pi[L - 1::-1][:L]
    st_r = rev_r[:, :, :, None] * bbr[None] - rev_i[:, :, :, None] * bbi[None]
    st_i = rev_r[:, :, :, None] * bbi[None] + rev_i[:, :, :, None] * bbr[None]
    w_st = jnp.concatenate([st_r, st_i], axis=2)
    w_st = w_st.transpose(1, 0, 3, 2).reshape(SSM_GROUPS, L * SSM_GROUP, 2 * SSM_STATE)
    w_out = jnp.concatenate([car[1:], -cai[1:]], axis=3)
    w_out = w_out.transpose(1, 3, 0, 2).reshape(SSM_GROUPS, 2 * SSM_STATE, L * SSM_GROUP)
    a_chunk = jnp.concatenate([pr[L], pi[L]], axis=1)
    return toep, w_st, w_out, a_chunk


def _ssm_local_kernel(u_ref, w_ref, o_ref):
    o_ref[0] = jnp.dot(u_ref[0], w_ref[0], preferred_element_type=jnp.float32)


def _ssm_scan_kernel(s_ref, p_ref, q_ref, o_ref):
    n_chunks = s_ref.shape[1]
    pm, qm = p_ref[...], q_ref[...]

    def body(c, h):
        o_ref[0, c] = h
        return h * pm + pltpu.roll(h, SSM_STATE, 1) * qm + s_ref[0, c]

    lax.fori_loop(0, n_chunks, body, jnp.zeros(pm.shape, jnp.float32))


def _ssm_out_kernel(u_ref, h_ref, t_ref, w_ref, o_ref):
    y = jnp.dot(u_ref[0], t_ref[0], preferred_element_type=jnp.float32)
    y += jnp.dot(h_ref[0].astype(jnp.bfloat16), w_ref[0], preferred_element_type=jnp.float32)
    o_ref[0] = y


def _ssm(u, toep, w_st, w_out, a_chunk, B, S):
    L, G, C16 = SSM_CHUNK, SSM_GROUPS, SSM_GROUP
    nc = S // L
    R = B * nc
    K = L * C16
    N2 = 2 * SSM_STATE
    bf = jnp.bfloat16
    ug = u.reshape(R, L, G, C16).transpose(2, 0, 1, 3).reshape(G, R, K).astype(bf)
    s_loc = pl.pallas_call(
        _ssm_local_kernel,
        grid=(G,),
        in_specs=[pl.BlockSpec((1, R, K), lambda g: (g, 0, 0)),
                  pl.BlockSpec((1, K, N2), lambda g: (g, 0, 0))],
        out_specs=pl.BlockSpec((1, R, N2), lambda g: (g, 0, 0)),
        out_shape=jax.ShapeDtypeStruct((G, R, N2), jnp.float32),
        compiler_params=_params("parallel"),
        name="ssm_local",
    )(ug, w_st.astype(bf))
    s_bc = s_loc.reshape(G, B, nc, N2).transpose(1, 2, 0, 3)
    ar, ai = a_chunk[:, :SSM_STATE], a_chunk[:, SSM_STATE:]
    pm = jnp.concatenate([ar, ar], axis=1)
    qm = jnp.concatenate([-ai, ai], axis=1)
    h_in = pl.pallas_call(
        _ssm_scan_kernel,
        grid=(B,),
        in_specs=[pl.BlockSpec((1, nc, G, N2), lambda b: (b, 0, 0, 0)),
                  pl.BlockSpec((G, N2), lambda b: (0, 0)),
                  pl.BlockSpec((G, N2), lambda b: (0, 0))],
        out_specs=pl.BlockSpec((1, nc, G, N2), lambda b: (b, 0, 0, 0)),
        out_shape=jax.ShapeDtypeStruct((B, nc, G, N2), jnp.float32),
        compiler_params=_params("parallel"),
        name="ssm_scan",
    )(s_bc, pm, qm)
    h_g = h_in.transpose(2, 0, 1, 3).reshape(G, R, N2)
    y_g = pl.pallas_call(
        _ssm_out_kernel,
        grid=(G,),
        in_specs=[pl.BlockSpec((1, R, K), lambda g: (g, 0, 0)),
                  pl.BlockSpec((1, R, N2), lambda g: (g, 0, 0)),
                  pl.BlockSpec((1, K, K), lambda g: (g, 0, 0)),
                  pl.BlockSpec((1, N2, K), lambda g: (g, 0, 0))],
        out_specs=pl.BlockSpec((1, R, K), lambda g: (g, 0, 0)),
        out_shape=jax.ShapeDtypeStruct((G, R, K), jnp.float32),
        compiler_params=_params("parallel"),
        name="ssm_out",
    )(ug, h_g, toep.astype(bf), w_out.astype(bf))
    return y_g.reshape(G, R, L, C16).transpose(1, 2, 0, 3).reshape(B * S, SSM_WIDTH)


def _route(logits):
    tm = logits.shape[0]
    lane = lax.broadcasted_iota(jnp.int32, (tm, LANES), 1)
    is_g = jnp.logical_and(lane >= N_EXPERTS, lane < N_EXPERTS + N_GROUPS)
    gl = jnp.where(is_g, logits, -jnp.inf)
    gmax = jnp.max(gl, axis=-1, keepdims=True)
    ge = jnp.exp(gl - gmax)
    gp = ge / jnp.sum(ge, axis=-1, keepdims=True)
    g_p = jnp.max(gp, axis=-1, keepdims=True)
    g_lane = jnp.min(jnp.where(jnp.logical_and(is_g, gp == g_p), lane, LANES), axis=-1, keepdims=True)
    g_idx = g_lane - N_EXPERTS
    grp_of_lane = lax.shift_right_logical(lane, int(math.log2(EXPERTS_PER_GROUP)))
    in_grp = jnp.logical_and(lane < N_EXPERTS, grp_of_lane == g_idx)
    el = jnp.where(in_grp, logits, -jnp.inf)
    v1 = jnp.max(el, axis=-1, keepdims=True)
    i1 = jnp.min(jnp.where(el == v1, lane, LANES), axis=-1, keepdims=True)
    el2 = jnp.where(lane == i1, -jnp.inf, el)
    v2 = jnp.max(el2, axis=-1, keepdims=True)
    i2 = jnp.min(jnp.where(el2 == v2, lane, LANES), axis=-1, keepdims=True)
    e2 = jnp.exp(v2 - v1)
    den = 1.0 + e2
    w1 = (1.0 / den) * g_p
    w2 = (e2 / den) * g_p
    return jnp.where(lane == i1, w1, jnp.where(lane == i2, w2, 0.0))


def _merge_kernel(y_ref, u_ref, att_ref, gs_ref, ga_ref, x_ref, d_ref, wglu_ref, bglu_ref,
                  wps_ref, wpa_ref, wo_ref, g2_ref, wr_ref, br_ref,
                  x1_ref, h2_ref, comb_ref):
    bf = jnp.bfloat16
    f32 = jnp.float32
    y = y_ref[...] + d_ref[...] * u_ref[...]
    z = jax.nn.gelu(y)
    gate = jnp.dot(z.astype(bf), wglu_ref[...], preferred_element_type=f32) + bglu_ref[...]
    s5 = z * jax.nn.sigmoid(gate)
    y_ssm = jnp.dot(s5.astype(bf), wps_ref[...], preferred_element_type=f32)
    y_att = jnp.dot(att_ref[...], wpa_ref[...], preferred_element_type=f32)
    merged = jax.nn.sigmoid(gs_ref[...]) * y_ssm + jax.nn.sigmoid(ga_ref[...]) * y_att
    x1 = x_ref[...] + jnp.dot(merged.astype(bf), wo_ref[...], preferred_element_type=f32)
    x1_ref[...] = x1
    ms = jnp.mean(x1 * x1, axis=-1, keepdims=True)
    h2 = x1 * lax.rsqrt(ms + EPS) * g2_ref[...]
    h2_ref[...] = h2.astype(bf)
    logits = _dot_split(h2, wr_ref[...], (((1,), (0,)), ((), ()))) + br_ref[...]
    comb_ref[...] = _route(logits)


def _merge(y_raw, proj, att, x2, d, w_glu, b_glu, w_ps, w_pa, w_o, g2, w_r, b_r, tm=256):
    T, D = x2.shape
    W = SSM_WIDTH
    row = lambda shape, col: pl.BlockSpec(shape, lambda i: (i, col))
    full = lambda a: pl.BlockSpec(a.shape, lambda i: (0,) * a.ndim)
    consts = [d.reshape(1, W), w_glu, b_glu.reshape(1, W), w_ps, w_pa, w_o, g2.reshape(1, D), w_r, b_r]
    return pl.pallas_call(
        _merge_kernel,
        grid=(T // tm,),
        in_specs=[row((tm, W), 0), row((tm, W), 0), row((tm, W), 0),
                  row((tm, D), 2), row((tm, D), 3), row((tm, D), 0)] + [full(c) for c in consts],
        out_specs=[row((tm, D), 0), row((tm, D), 0), row((tm, LANES), 0)],
        out_shape=[jax.ShapeDtypeStruct((T, D), jnp.float32),
                   jax.ShapeDtypeStruct((T, D), jnp.bfloat16),
                   jax.ShapeDtypeStruct((T, LANES), jnp.float32)],
        compiler_params=_params("parallel"),
        name="merge",
    )(y_raw, proj, att, proj, proj, x2, *consts)


def _moe_kernel(h_ref, comb_ref, x_ref, wg_ref, wu_ref, wd_ref, o_ref, act_ref):
    f32 = jnp.float32
    h = h_ref[...]
    comb = comb_ref[...]
    step = 4 * D_EXPERT
    for c in range(N_EXPERTS * D_EXPERT // step):
        hg = jnp.dot(h, wg_ref[:, c * step:(c + 1) * step], preferred_element_type=f32)
        hu = jnp.dot(h, wu_ref[:, c * step:(c + 1) * step], preferred_element_type=f32)
        a = jax.nn.silu(hg) * hu
        for j in range(step // D_EXPERT):
            e = c * (step // D_EXPERT) + j
            act_ref[:, e * D_EXPERT:(e + 1) * D_EXPERT] = (
                a[:, j * D_EXPERT:(j + 1) * D_EXPERT] * comb[:, e:e + 1]).astype(act_ref.dtype)
    o_ref[...] = x_ref[...] + jnp.dot(act_ref[...], wd_ref[...], preferred_element_type=f32)


def _moe(h2, comb, x1, wg, wu, wd, tm=512):
    T, D = x1.shape
    F = N_EXPERTS * D_EXPERT
    row = lambda w: pl.BlockSpec((tm, w), lambda i: (i, 0))
    once = lambda a: pl.BlockSpec(a.shape, lambda i: (0, 0), pipeline_mode=pl.Buffered(1))
    return pl.pallas_call(
        _moe_kernel,
        grid=(T // tm,),
        in_specs=[row(D), row(LANES), row(D), once(wg), once(wu), once(wd)],
        out_specs=row(D),
        out_shape=jax.ShapeDtypeStruct((T, D), jnp.float32),
        scratch_shapes=[pltpu.VMEM((tm, F), jnp.bfloat16)],
        compiler_params=_params("parallel"),
        name="moe",
    )(h2, comb, x1, wg, wu, wd)


def _layer(x, norm1_g, w_in, lam_re, lam_im, log_dt, ssm_b_re, ssm_b_im, ssm_c_re, ssm_c_im,
           ssm_d, w_glu, b_glu, q_norm_g, k_norm_g, w_proj_ssm, w_proj_attn, w_out, norm2_g,
           w_router_group, b_router_group, w_router_expert, b_router_expert, w_gate, w_up, w_down):
    B, S, D = x.shape
    bf = jnp.bfloat16
    x2 = x.reshape(B * S, D)
    proj = _in_proj(x2, norm1_g, w_in.astype(bf))
    q, k_bf, v_bf, kmean = _qk_prep(proj, q_norm_g, k_norm_g, B, S)
    att = _moba(q, k_bf, v_bf, kmean.reshape(B, S // MOBA_BLOCK, ATTN_WIDTH))
    toep, w_st, w_o_ssm, a_chunk = _ssm_weights(lam_re, lam_im, log_dt, ssm_b_re, ssm_b_im,
                                               ssm_c_re, ssm_c_im)
    y_raw = _ssm(proj[:, :SSM_WIDTH], toep, w_st, w_o_ssm, a_chunk, B, S)
    pad = jnp.zeros((D, LANES - N_EXPERTS - N_GROUPS), jnp.float32)
    w_r = jnp.concatenate([w_router_expert, w_router_group, pad], axis=1)
    b_r = jnp.concatenate([b_router_expert, b_router_group,
                           jnp.zeros((LANES - N_EXPERTS - N_GROUPS,), jnp.float32)]).reshape(1, LANES)
    x1, h2, comb = _merge(y_raw, proj, att.reshape(B * S, ATTN_WIDTH), x2, ssm_d,
                          w_glu.astype(bf), b_glu, w_proj_ssm.astype(bf), w_proj_attn.astype(bf),
                          w_out.astype(bf), norm2_g, w_r, b_r)
    F = N_EXPERTS * D_EXPERT
    wg = w_gate.transpose(1, 0, 2).reshape(D, F).astype(bf)
    wu = w_up.transpose(1, 0, 2).reshape(D, F).astype(bf)
    wd = w_down.reshape(F, D).astype(bf)
    out = _moe(h2, comb, x1, wg, wu, wd)
    return out.reshape(B, S, D)


def kernel(x, norm1_g, w_in, lam_re, lam_im, log_dt, ssm_b_re, ssm_b_im, ssm_c_re, ssm_c_im,
           ssm_d, w_glu, b_glu, q_norm_g, k_norm_g, w_proj_ssm, w_proj_attn, w_out, norm2_g,
           w_router_group, b_router_group, w_router_expert, b_router_expert, w_gate, w_up, w_down):
    args = (norm1_g, w_in, lam_re, lam_im, log_dt, ssm_b_re, ssm_b_im, ssm_c_re, ssm_c_im,
            ssm_d, w_glu, b_glu, q_norm_g, k_norm_g, w_proj_ssm, w_proj_attn, w_out, norm2_g,
            w_router_group, b_router_group, w_router_expert, b_router_expert, w_gate, w_up, w_down)
    for l in range(w_in.shape[0]):
        x = _layer(x, *(a[l] for a in args))
    return x
```

```python
import functools
import math

import jax
import jax.numpy as jnp
from jax import lax
from jax.experimental import pallas as pl
from jax.experimental.pallas import tpu as pltpu

D_MODEL = 1024
SSM_WIDTH = 512
SSM_GROUP = 16
SSM_GROUPS = SSM_WIDTH // SSM_GROUP
SSM_STATE = 64
N_HEADS = 8
HEAD_DIM = 64
ATTN_WIDTH = N_HEADS * HEAD_DIM
MOBA_BLOCK = 256
MOBA_TOPK = 3
ROPE_THETA = 500000.0
ROPE_DIM = HEAD_DIM // 4
N_GROUPS = 4
EXPERTS_PER_GROUP = 8
N_EXPERTS = N_GROUPS * EXPERTS_PER_GROUP
D_EXPERT = 128
EPS = 1e-6
IN_WIDTH = SSM_WIDTH + 3 * ATTN_WIDTH + 2 * D_MODEL

LANES = 128
SUBLANES = 8
VMEM_LIMIT = 56 * 1024 * 1024

SSM_CHUNK = 16
SSM_ROW_BLOCK = 128
NEG_BIG = -1e30
MOBA_KV_CHUNK = 1024

_NT = (((1,), (1,)), ((), ()))


def _params(*sem):
    return pltpu.CompilerParams(dimension_semantics=sem, vmem_limit_bytes=VMEM_LIMIT)


def _dot_split(a, b, dims):
    bf, f32 = jnp.bfloat16, jnp.float32
    a_hi, b_hi = a.astype(bf), b.astype(bf)
    a_lo = (a - a_hi.astype(f32)).astype(bf)
    b_lo = (b - b_hi.astype(f32)).astype(bf)
    dot = lambda x, y: lax.dot_general(x, y, dims, preferred_element_type=f32)
    return dot(a_hi, b_hi) + (dot(a_hi, b_lo) + dot(a_lo, b_hi))


def _in_proj_kernel(x_ref, g_ref, w_ref, o_ref):
    x = x_ref[...]
    ms = jnp.mean(x * x, axis=-1, keepdims=True)
    h = (x * lax.rsqrt(ms + EPS) * g_ref[...]).astype(jnp.bfloat16)
    n_out = o_ref.shape[1]
    step = 512
    for c in range(n_out // step):
        o_ref[:, c * step:(c + 1) * step] = jnp.dot(
            h, w_ref[:, c * step:(c + 1) * step], preferred_element_type=jnp.float32)


def _in_proj(x2, g, w_bf, tm=256):
    T, D = x2.shape
    N = w_bf.shape[1]
    return pl.pallas_call(
        _in_proj_kernel,
        grid=(T // tm,),
        in_specs=[pl.BlockSpec((tm, D), lambda i: (i, 0)),
                  pl.BlockSpec((1, D), lambda i: (0, 0)),
                  pl.BlockSpec((D, N), lambda i: (0, 0))],
        out_specs=pl.BlockSpec((tm, N), lambda i: (i, 0)),
        out_shape=jax.ShapeDtypeStruct((T, N), jnp.float32),
        compiler_params=_params("parallel"),
        name="in_proj",
    )(x2, g.reshape(1, D), w_bf)


def _head_norm_rope(x, g, ct, sa, sb):
    ts = x.shape[0]
    lane = lax.broadcasted_iota(jnp.int32, (ts, LANES), 1)
    lo = lane < HEAD_DIM
    outs = []
    for t in range(x.shape[1] // LANES):
        xt = x[:, t * LANES:(t + 1) * LANES]
        sq = xt * xt
        s_lo = jnp.sum(jnp.where(lo, sq, 0.0), axis=-1, keepdims=True)
        s_hi = jnp.sum(jnp.where(lo, 0.0, sq), axis=-1, keepdims=True)
        ms = jnp.where(lo, s_lo, s_hi) * (1.0 / HEAD_DIM)
        y = xt * lax.rsqrt(ms + EPS) * g
        y = y * ct + pltpu.roll(y, LANES - ROPE_DIM // 2, 1) * sa + pltpu.roll(y, ROPE_DIM // 2, 1) * sb
        outs.append(y)
    return jnp.concatenate(outs, axis=1)


def _qk_prep_kernel(q_ref, k_ref, v_ref, gq_ref, gk_ref, ct_ref, sa_ref, sb_ref,
                    qo_ref, ko_ref, vo_ref, km_ref):
    ct, sa, sb = ct_ref[...], sa_ref[...], sb_ref[...]
    qn = _head_norm_rope(q_ref[...], gq_ref[...], ct, sa, sb)
    kn = _head_norm_rope(k_ref[...], gk_ref[...], ct, sa, sb)
    qo_ref[0] = qn
    ko_ref[0] = kn.astype(jnp.bfloat16)
    vo_ref[0] = v_ref[...].T.astype(jnp.bfloat16)
    km_ref[0, 0] = jnp.mean(kn, axis=0, keepdims=True)


def _rope_tables(S):
    half = ROPE_DIM // 2
    inv_freq = ROPE_THETA ** (-jnp.arange(half, dtype=jnp.float32) * 2.0 / ROPE_DIM)
    ang = jnp.arange(S, dtype=jnp.float32)[:, None] * inv_freq[None, :]
    cos, sin = jnp.cos(ang), jnp.sin(ang)
    ones = jnp.ones((S, HEAD_DIM - ROPE_DIM), jnp.float32)
    zeros = jnp.zeros((S, HEAD_DIM - ROPE_DIM), jnp.float32)
    zh = jnp.zeros((S, half), jnp.float32)
    ct = jnp.concatenate([cos, cos, ones], axis=1)
    sa = jnp.concatenate([-sin, zh, zeros], axis=1)
    sb = jnp.concatenate([zh, sin, zeros], axis=1)
    tile = lambda t: jnp.concatenate([t, t], axis=1)
    return tile(ct), tile(sa), tile(sb)


def _qk_prep(proj, gq, gk, B, S):
    ts = MOBA_BLOCK
    nb = S // ts
    W = ATTN_WIDTH
    ct, sa, sb = _rope_tables(S)
    g128 = lambda g: jnp.concatenate([g, g]).reshape(1, LANES)
    row = lambda b, s: b * nb + s
    tab = pl.BlockSpec((ts, LANES), lambda b, s: (s, 0))
    out3 = pl.BlockSpec((1, ts, W), lambda b, s: (b, s, 0))
    return pl.pallas_call(
        _qk_prep_kernel,
        grid=(B, nb),
        in_specs=[pl.BlockSpec((ts, W), lambda b, s: (row(b, s), 1)),
                  pl.BlockSpec((ts, W), lambda b, s: (row(b, s), 2)),
                  pl.BlockSpec((ts, W), lambda b, s: (row(b, s), 3)),
                  pl.BlockSpec((1, LANES), lambda b, s: (0, 0)),
                  pl.BlockSpec((1, LANES), lambda b, s: (0, 0)),
                  tab, tab, tab],
        out_specs=[out3, out3, pl.BlockSpec((1, W, ts), lambda b, s: (b, 0, s)),
                   pl.BlockSpec((1, 1, 1, W), lambda b, s: (b, s, 0, 0))],
        out_shape=[jax.ShapeDtypeStruct((B, S, W), jnp.float32),
                   jax.ShapeDtypeStruct((B, S, W), jnp.bfloat16),
                   jax.ShapeDtypeStruct((B, W, S), jnp.bfloat16),
                   jax.ShapeDtypeStruct((B, nb, 1, W), jnp.float32)],
        compiler_params=_params("parallel", "parallel"),
        name="qk_prep",
    )(proj, proj, proj, g128(gq), g128(gk), ct, sa, sb)


def _moba_kernel(q_ref, k_ref, vt_ref, km_ref, oh_ref, o_ref, s_ref):
    i = pl.program_id(2)
    tq = q_ref.shape[1]
    nb = km_ref.shape[1]
    bf = jnp.bfloat16
    f32 = jnp.float32
    q = q_ref[0]
    km = km_ref[0]
    lane = lax.broadcasted_iota(jnp.int32, (tq, LANES), 1)
    blk = lax.broadcasted_iota(jnp.int32, (nb, tq), 0)
    past = blk < i
    scale = HEAD_DIM ** -0.5 * math.log2(math.e)

    q_aug = []
    for h in range(2):
        hmask = (lane < HEAD_DIM) if h == 0 else (lane >= HEAD_DIM)
        qh = jnp.where(hmask, q, 0.0)
        s = _dot_split(km, qh, _NT)
        s = jnp.where(past, s, -jnp.inf)
        sel = jnp.zeros((nb, tq), jnp.bool_)
        for _ in range(MOBA_TOPK):
            mx = jnp.max(s, axis=0, keepdims=True)
            first = jnp.min(jnp.where(s == mx, blk, nb), axis=0, keepdims=True)
            pick = blk == first
            sel = jnp.logical_or(sel, jnp.logical_and(pick, past))
            s = jnp.where(pick, -jnp.inf, s)
        bias_t = jnp.where(sel, 0.0, NEG_BIG)
        bias_t = jnp.concatenate([bias_t, jnp.zeros((LANES - nb, tq), f32)], axis=0)
        bias = bias_t.T.astype(bf)
        q_aug.append(jnp.concatenate([(qh * scale).astype(bf), bias], axis=1))

    ck = MOBA_KV_CHUNK
    n_chunks = lax.shift_right_logical(i + (ck // tq - 1), int(math.log2(ck // tq)))
    n_chunks = jnp.maximum(n_chunks, 1)

    def logits(c, slot):
        off = pl.multiple_of(c * ck, ck)
        k_aug = jnp.concatenate([k_ref[0, pl.ds(off, ck), :], oh_ref[pl.ds(off, ck), :]], axis=1)
        cm = []
        for h in range(2):
            s = lax.dot_general(k_aug, q_aug[h], _NT, preferred_element_type=f32)
            s_ref[slot, h] = s
            cm.append(jnp.max(s, axis=0, keepdims=True))
        return cm

    def weigh(s, vt, cm, m, l, acc):
        m_new = cm if m is None else jnp.maximum(m, cm)
        p = jnp.exp2(s - m_new)
        psum = jnp.sum(p, axis=0, keepdims=True)
        pv = jnp.dot(vt, p.astype(bf), preferred_element_type=f32)
        if m is None:
            return [m_new, psum, pv]
        alpha = jnp.exp2(m - m_new)
        return [m_new, alpha * l + psum, alpha * acc + pv]

    def weigh_chunk(c, slot, cm, state):
        off = pl.multiple_of(c * ck, ck)
        out = []
        for h in range(2):
            vt = vt_ref[0, h * HEAD_DIM:(h + 1) * HEAD_DIM, pl.ds(off, ck)]
            out += weigh(s_ref[slot, h], vt, cm[h], *state[3 * h:3 * h + 3])
        return out

    off_d = pl.multiple_of(i * tq, tq)
    kd = k_ref[0, pl.ds(off_d, tq), :]
    key_i = lax.broadcasted_iota(jnp.int32, (tq, tq), 0)
    qry_i = lax.broadcasted_iota(jnp.int32, (tq, tq), 1)
    causal = key_i <= qry_i
    state = []
    for h in range(2):
        s = lax.dot_general(kd, q_aug[h][:, :LANES], _NT, preferred_element_type=f32)
        s = jnp.where(causal, s, -jnp.inf)
        vt = vt_ref[0, h * HEAD_DIM:(h + 1) * HEAD_DIM, pl.ds(off_d, tq)]
        state += weigh(s, vt, jnp.max(s, axis=0, keepdims=True), None, None, None)

    cm0 = logits(0, 0)

    def body(c, carry):
        cm, state = carry[:2], carry[2:]
        state = weigh_chunk(c, c & 1, cm, state)
        cm_next = logits(c + 1, (c + 1) & 1)
        return tuple(cm_next) + tuple(state)

    carry = lax.fori_loop(0, n_chunks - 1, body, tuple(cm0) + tuple(state))
    last = n_chunks - 1
    _, l0, a0, _, l1, a1 = weigh_chunk(last, last & 1, carry[:2], carry[2:])
    o_t = jnp.concatenate([a0 / l0, a1 / l1], axis=0)
    o_ref[0] = o_t.T.astype(o_ref.dtype)


def _moba(q, k_bf, vt_bf, kmean):
    B, S, W = q.shape
    tq = MOBA_BLOCK
    nb = S // tq
    onehot = (jnp.arange(S)[:, None] // tq == jnp.arange(LANES)[None, :]).astype(jnp.bfloat16)
    qo = pl.BlockSpec((1, tq, LANES), lambda b, hp, i: (b, i, hp))
    return pl.pallas_call(
        _moba_kernel,
        grid=(B, W // LANES, nb),
        in_specs=[qo,
                  pl.BlockSpec((1, S, LANES), lambda b, hp, i: (b, 0, hp)),
                  pl.BlockSpec((1, LANES, S), lambda b, hp, i: (b, hp, 0)),
                  pl.BlockSpec((1, nb, LANES), lambda b, hp, i: (b, 0, hp)),
                  pl.BlockSpec((S, LANES), lambda b, hp, i: (0, 0))],
        out_specs=qo,
        out_shape=jax.ShapeDtypeStruct((B, S, W), jnp.bfloat16),
        scratch_shapes=[pltpu.VMEM((2, 2, MOBA_KV_CHUNK, tq), jnp.float32)],
        compiler_params=_params("parallel", "parallel", "arbitrary"),
        name="moba",
    )(q, k_bf, vt_bf, kmean, onehot)


def _ssm_weights(lam_re, lam_im, log_dt, b_re, b_im, c_re, c_im):
    L = SSM_CHUNK
    f32 = jnp.float32
    dt = jnp.exp(log_dt.astype(f32))[:, None]
    lr = jnp.minimum(lam_re.astype(f32), -1e-4)
    li = lam_im.astype(f32)
    mag = jnp.exp(lr * dt)
    ar = mag * jnp.cos(li * dt)
    ai = mag * jnp.sin(li * dt)
    den = lr * lr + li * li
    nr, ni = ar - 1.0, ai
    cr = (nr * lr + ni * li) / den
    ci = (ni * lr - nr * li) / den
    brf, bif = b_re.astype(f32), b_im.astype(f32)
    bbr = cr[..., None] * brf - ci[..., None] * bif
    bbi = cr[..., None] * bif + ci[..., None] * brf
    tau = jnp.arange(L + 1, dtype=f32)[:, None, None]
    pmag = jnp.exp(tau * (lr * dt)[None])
    pr = pmag * jnp.cos(tau * (li * dt)[None])
    pi = pmag * jnp.sin(tau * (li * dt)[None])
    cre, cim = c_re.astype(f32), c_im.astype(f32)
    hi = lax.Precision.HIGHEST
    car = cre[None] * pr[:, :, None, :] - cim[None] * pi[:, :, None, :]
    cai = cre[None] * pi[:, :, None, :] + cim[None] * pr[:, :, None, :]
    kern = (jnp.einsum('tgon,gni->tgoi', car[:L], bbr, precision=hi)
            - jnp.einsum('tgon,gni->tgoi', cai[:L], bbi, precision=hi))
    bf = jnp.bfloat16
    K = L * SSM_GROUP
    lag = jnp.arange(L)[:, None] - jnp.arange(L)[None, :]
    toep = jnp.where((lag >= 0)[:, :, None, None, None],
                     kern.astype(bf)[jnp.clip(lag, 0, L - 1)], 0)
    toep_t = toep.transpose(2, 0, 3, 1, 4).reshape(SSM_GROUPS, K, K)
    rev_r, rev_i = pr[L - 1::-1][:L], pi[L - 1::-1][:L]
    st_r = rev_r[:, :, :, None] * bbr[None] - rev_i[:, :, :, None] * bbi[None]
    st_i = rev_r[:, :, :, None] * bbi[None] + rev_i[:, :, :, None] * bbr[None]
    w_st = jnp.concatenate([st_r, st_i], axis=2)
    w_st_t = w_st.transpose(1, 2, 0, 3).reshape(SSM_GROUPS, 2 * SSM_STATE, K).astype(bf)
    w_out = jnp.concatenate([car[1:], -cai[1:]], axis=3)
    w_out_t = w_out.transpose(1, 0, 2, 3).reshape(SSM_GROUPS, K, 2 * SSM_STATE).astype(bf)
    a_chunk = jnp.concatenate([pr[L], pi[L]], axis=1)
    return toep_t, w_st_t, w_out_t, a_chunk


def _ssm_in_kernel(*refs):
    L, G, C = SSM_CHUNK, SSM_GROUPS, SSM_GROUP
    n_tiles = SSM_WIDTH // LANES
    u_refs, (w_ref, ut_ref, s_ref) = refs[:n_tiles], refs[n_tiles:]
    rb = ut_ref.shape[2]
    gq = LANES // C
    for q, u_ref in enumerate(u_refs):
        for s in range(L):
            xs = u_ref[pl.ds(s, rb, stride=L), :]
            ut_ref[q * gq:(q + 1) * gq, s * C:(s + 1) * C, :] = (
                xs.T.reshape(gq, C, rb).astype(ut_ref.dtype))

    for g in range(G):
        st = jnp.dot(w_ref[g], ut_ref[g], preferred_element_type=jnp.float32)
        s_ref[g] = st.T


def _ssm_scan_kernel(s_ref, p_ref, q_ref, o_ref):
    n_chunks = s_ref.shape[1]
    pm, qm = p_ref[...], q_ref[...]

    def body(c, carry):
        h, hs = carry
        o_ref[0, c] = h
        s = s_ref[0, c]
        return (h * pm + hs * qm + s, hs * pm - h * qm + pltpu.roll(s, SSM_STATE, 1))

    zero = jnp.zeros(pm.shape, jnp.float32)
    lax.fori_loop(0, n_chunks, body, (zero, zero), unroll=8)


def _ssm_out_kernel(ut_ref, h_ref, t_ref, w_ref, y_ref, yt_ref):
    L, G, C = SSM_CHUNK, SSM_GROUPS, SSM_GROUP
    rb = ut_ref.shape[2]

    for g in range(G):
        yt = jnp.dot(t_ref[g], ut_ref[g], preferred_element_type=jnp.float32)
        yt += lax.dot_general(w_ref[g], h_ref[g].astype(jnp.bfloat16), _NT,
                              preferred_element_type=jnp.float32)
        yt_ref[g] = yt
    gq = LANES // C
    for q in range(SSM_WIDTH // LANES):
        for t in range(L):
            z = yt_ref[q * gq:(q + 1) * gq, t * C:(t + 1) * C, :].reshape(LANES, rb)
            y_ref[q, pl.ds(t, rb, stride=L), :] = z.T


def _ssm(u_src, toep_t, w_st_t, w_out_t, a_chunk, B, S):
    L, G = SSM_CHUNK, SSM_GROUPS
    nc = S // L
    R = B * nc
    K = L * SSM_GROUP
    N2 = 2 * SSM_STATE
    n_tiles = SSM_WIDTH // LANES
    rb = min(SSM_ROW_BLOCK, R)
    bf = jnp.bfloat16
    ut, s_loc = pl.pallas_call(
        _ssm_in_kernel,
        grid=(R // rb,),
        in_specs=[pl.BlockSpec((rb * L, LANES), functools.partial(lambda q, j: (j, q), q))
                  for q in range(n_tiles)] + [pl.BlockSpec((G, N2, K), lambda j: (0, 0, 0))],
        out_specs=[pl.BlockSpec((G, K, rb), lambda j: (0, 0, j)),
                   pl.BlockSpec((G, rb, N2), lambda j: (0, j, 0))],
        out_shape=[jax.ShapeDtypeStruct((G, K, R), bf),
                   jax.ShapeDtypeStruct((G, R, N2), jnp.float32)],
        compiler_params=_params("parallel"),
        name="ssm_in",
    )(*([u_src] * n_tiles), w_st_t)
    s_bc = s_loc.reshape(G, B, nc, N2).transpose(1, 2, 0, 3)
    ar, ai = a_chunk[:, :SSM_STATE], a_chunk[:, SSM_STATE:]
    pm = jnp.concatenate([ar, ar], axis=1)
    qm = jnp.concatenate([-ai, ai], axis=1)
    h_in = pl.pallas_call(
        _ssm_scan_kernel,
        grid=(B,),
        in_specs=[pl.BlockSpec((1, nc, G, N2), lambda b: (b, 0, 0, 0)),
                  pl.BlockSpec((G, N2), lambda b: (0, 0)),
                  pl.BlockSpec((G, N2), lambda b: (0, 0))],
        out_specs=pl.BlockSpec((1, nc, G, N2), lambda b: (b, 0, 0, 0)),
        out_shape=jax.ShapeDtypeStruct((B, nc, G, N2), jnp.float32),
        compiler_params=_params("parallel"),
        name="ssm_scan",
    )(s_bc, pm, qm)
    h_g = h_in.transpose(2, 0, 1, 3).reshape(G, R, N2)
    return pl.pallas_call(
        _ssm_out_kernel,
        grid=(R // rb,),
        in_specs=[pl.BlockSpec((G, K, rb), lambda j: (0, 0, j)),
                  pl.BlockSpec((G, rb, N2), lambda j: (0, j, 0)),
                  pl.BlockSpec((G, K, K), lambda j: (0, 0, 0)),
                  pl.BlockSpec((G, K, N2), lambda j: (0, 0, 0))],
        out_specs=pl.BlockSpec((n_tiles, rb * L, LANES), lambda j: (0, j, 0)),
        out_shape=jax.ShapeDtypeStruct((n_tiles, B * S, LANES), jnp.float32),
        scratch_shapes=[pltpu.VMEM((G, K, rb), jnp.float32)],
        compiler_params=_params("parallel"),
        name="ssm_out",
    )(ut, h_g, toep_t, w_out_t)


def _route(logits):
    tm = logits.shape[0]
    lane = lax.broadcasted_iota(jnp.int32, (tm, LANES), 1)
    is_g = jnp.logical_and(lane >= N_EXPERTS, lane < N_EXPERTS + N_GROUPS)
    gl = jnp.where(is_g, logits, -jnp.inf)
    gmax = jnp.max(gl, axis=-1, keepdims=True)
    ge = jnp.exp(gl - gmax)
    gp = ge / jnp.sum(ge, axis=-1, keepdims=True)
    g_p = jnp.max(gp, axis=-1, keepdims=True)
    g_lane = jnp.min(jnp.where(jnp.logical_and(is_g, gp == g_p), lane, LANES), axis=-1, keepdims=True)
    g_idx = g_lane - N_EXPERTS
    grp_of_lane = lax.shift_right_logical(lane, int(math.log2(EXPERTS_PER_GROUP)))
    in_grp = jnp.logical_and(lane < N_EXPERTS, grp_of_lane == g_idx)
    el = jnp.where(in_grp, logits, -jnp.inf)
    v1 = jnp.max(el, axis=-1, keepdims=True)
    i1 = jnp.min(jnp.where(el == v1, lane, LANES), axis=-1, keepdims=True)
    el2 = jnp.where(lane == i1, -jnp.inf, el)
    v2 = jnp.max(el2, axis=-1, keepdims=True)
    i2 = jnp.min(jnp.where(el2 == v2, lane, LANES), axis=-1, keepdims=True)
    e2 = jnp.exp(v2 - v1)
    den = 1.0 + e2
    w1 = (1.0 / den) * g_p
    w2 = (e2 / den) * g_p
    return jnp.where(lane == i1, w1, jnp.where(lane == i2, w2, 0.0))


def _merge_kernel(y_ref, u_ref, att_ref, gs_ref, ga_ref, x_ref, d_ref, wglu_ref, bglu_ref,
                  wps_ref, wpa_ref, wo_ref, g2_ref, wr_ref, br_ref,
                  x1_ref, h2_ref, comb_ref):
    bf = jnp.bfloat16
    f32 = jnp.float32
    y_tiles = [y_ref[q] for q in range(y_ref.shape[0])]
    y = jnp.concatenate(y_tiles, axis=1) + d_ref[...] * u_ref[...]
    z = jax.nn.gelu(y)
    gate = jnp.dot(z.astype(bf), wglu_ref[...], preferred_element_type=f32) + bglu_ref[...]
    s5 = z * jax.nn.sigmoid(gate)
    y_ssm = jnp.dot(s5.astype(bf), wps_ref[...], preferred_element_type=f32)
    y_att = jnp.dot(att_ref[...], wpa_ref[...], preferred_element_type=f32)
    merged = jax.nn.sigmoid(gs_ref[...]) * y_ssm + jax.nn.sigmoid(ga_ref[...]) * y_att
    x1 = x_ref[...] + jnp.dot(merged.astype(bf), wo_ref[...], preferred_element_type=f32)
    x1_ref[...] = x1
    ms = jnp.mean(x1 * x1, axis=-1, keepdims=True)
    h2 = x1 * lax.rsqrt(ms + EPS) * g2_ref[...]
    h2_ref[...] = h2.astype(bf)
    logits = _dot_split(h2, wr_ref[...], (((1,), (0,)), ((), ()))) + br_ref[...]
    comb_ref[...] = _route(logits)


def _merge(y_raw, proj, att, x2, d, w_glu, b_glu, w_ps, w_pa, w_o, g2, w_r, b_r, tm=256):
    T, D = x2.shape
    W = SSM_WIDTH
    row = lambda shape, col: pl.BlockSpec(shape, lambda i: (i, col))
    full = lambda a: pl.BlockSpec(a.shape, lambda i: (0,) * a.ndim)
    consts = [d.reshape(1, W), w_glu, b_glu.reshape(1, W), w_ps, w_pa, w_o, g2.reshape(1, D), w_r, b_r]
    return pl.pallas_call(
        _merge_kernel,
        grid=(T // tm,),
        in_specs=[pl.BlockSpec((W // LANES, tm, LANES), lambda i: (0, i, 0)), row((tm, W), 0), row((tm, W), 0),
                  row((tm, D), 2), row((tm, D), 3), row((tm, D), 0)] + [full(c) for c in consts],
        out_specs=[row((tm, D), 0), row((tm, D), 0), row((tm, LANES), 0)],
        out_shape=[jax.ShapeDtypeStruct((T, D), jnp.float32),
                   jax.ShapeDtypeStruct((T, D), jnp.bfloat16),
                   jax.ShapeDtypeStruct((T, LANES), jnp.float32)],
        compiler_params=_params("parallel"),
        name="merge",
    )(y_raw, proj, att, proj, proj, x2, *consts)


def _moe_kernel(h_ref, comb_ref, x_ref, wg_ref, wu_ref, wd_ref, o_ref, act_ref):
    f32 = jnp.float32
    h = h_ref[...]
    comb = comb_ref[...]
    step = 4 * D_EXPERT
    for c in range(N_EXPERTS * D_EXPERT // step):
        hg = jnp.dot(h, wg_ref[:, c * step:(c + 1) * step], preferred_element_type=f32)
        hu = jnp.dot(h, wu_ref[:, c * step:(c + 1) * step], preferred_element_type=f32)
        a = jax.nn.silu(hg) * hu
        for j in range(step // D_EXPERT):
            e = c * (step // D_EXPERT) + j
            act_ref[:, e * D_EXPERT:(e + 1) * D_EXPERT] = (
                a[:, j * D_EXPERT:(j + 1) * D_EXPERT] * comb[:, e:e + 1]).astype(act_ref.dtype)
    o_ref[...] = x_ref[...] + jnp.dot(act_ref[...], wd_ref[...], preferred_element_type=f32)


def _moe(h2, comb, x1, wg, wu, wd, tm=512):
    T, D = x1.shape
    F = N_EXPERTS * D_EXPERT
    row = lambda w: pl.BlockSpec((tm, w), lambda i: (i, 0))
    once = lambda a: pl.BlockSpec(a.shape, lambda i: (0, 0), pipeline_mode=pl.Buffered(1))
    return pl.pallas_call(
        _moe_kernel,
        grid=(T // tm,),
        in_specs=[row(D), row(LANES), row(D), once(wg), once(wu), once(wd)],
        out_specs=row(D),
        out_shape=jax.ShapeDtypeStruct((T, D), jnp.float32),
        scratch_shapes=[pltpu.VMEM((tm, F), jnp.bfloat16)],
        compiler_params=_params("parallel"),
        name="moe",
    )(h2, comb, x1, wg, wu, wd)


def _layer(x, norm1_g, w_in, lam_re, lam_im, log_dt, ssm_b_re, ssm_b_im, ssm_c_re, ssm_c_im,
           ssm_d, w_glu, b_glu, q_norm_g, k_norm_g, w_proj_ssm, w_proj_attn, w_out, norm2_g,
           w_router_group, b_router_group, w_router_expert, b_router_expert, w_gate, w_up, w_down):
    B, S, D = x.shape
    bf = jnp.bfloat16
    x2 = x.reshape(B * S, D)
    proj = _in_proj(x2, norm1_g, w_in.astype(bf))
    q, k_bf, v_bf, kmean = _qk_prep(proj, q_norm_g, k_norm_g, B, S)
    att = _moba(q, k_bf, v_bf, kmean.reshape(B, S // MOBA_BLOCK, ATTN_WIDTH))
    toep, w_st, w_o_ssm, a_chunk = _ssm_weights(lam_re, lam_im, log_dt, ssm_b_re, ssm_b_im,
                                               ssm_c_re, ssm_c_im)
    y_raw = _ssm(proj, toep, w_st, w_o_ssm, a_chunk, B, S)
    pad = jnp.zeros((D, LANES - N_EXPERTS - N_GROUPS), jnp.float32)
    w_r = jnp.concatenate([w_router_expert, w_router_group, pad], axis=1)
    b_r = jnp.concatenate([b_router_expert, b_router_group,
                           jnp.zeros((LANES - N_EXPERTS - N_GROUPS,), jnp.float32)]).reshape(1, LANES)
    x1, h2, comb = _merge(y_raw, proj, att.reshape(B * S, ATTN_WIDTH), x2, ssm_d,
                          w_glu.astype(bf), b_glu, w_proj_ssm.astype(bf), w_proj_attn.astype(bf),
                          w_out.astype(bf), norm2_g, w_r, b_r)
    F = N_EXPERTS * D_EXPERT
    wg = w_gate.transpose(1, 0, 2).reshape(D, F).astype(bf)
    wu = w_up.transpose(1, 0, 2).reshape(D, F).astype(bf)
    wd = w_down.reshape(F, D).astype(bf)
    out = _moe(h2, comb, x1, wg, wu, wd)
    return out.reshape(B, S, D)


def kernel(x, norm1_g, w_in, lam_re, lam_im, log_dt, ssm_b_re, ssm_b_im, ssm_c_re, ssm_c_im,
           ssm_d, w_glu, b_glu, q_norm_g, k_norm_g, w_proj_ssm, w_proj_attn, w_out, norm2_g,
           w_router_group, b_router_group, w_router_expert, b_router_expert, w_gate, w_up, w_down):
    args = (norm1_g, w_in, lam_re, lam_im, log_dt, ssm_b_re, ssm_b_im, ssm_c_re, ssm_c_im,
            ssm_d, w_glu, b_glu, q_norm_g, k_norm_g, w_proj_ssm, w_proj_attn, w_out, norm2_g,
            w_router_group, b_router_group, w_router_expert, b_router_expert, w_gate, w_up, w_down)
    for l in range(w_in.shape[0]):
        x = _layer(x, *(a[l] for a in args))
    return x
```

```python
import functools
import math

import jax
import jax.numpy as jnp
from jax import lax
from jax.experimental import pallas as pl
from jax.experimental.pallas import tpu as pltpu

D_MODEL = 1024
SSM_WIDTH = 512
SSM_GROUP = 16
SSM_GROUPS = SSM_WIDTH // SSM_GROUP
SSM_STATE = 64
N_HEADS = 8
HEAD_DIM = 64
ATTN_WIDTH = N_HEADS * HEAD_DIM
MOBA_BLOCK = 256
MOBA_TOPK = 3
ROPE_THETA = 500000.0
ROPE_DIM = HEAD_DIM // 4
N_GROUPS = 4
EXPERTS_PER_GROUP = 8
N_EXPERTS = N_GROUPS * EXPERTS_PER_GROUP
D_EXPERT = 128
EPS = 1e-6
IN_WIDTH = SSM_WIDTH + 3 * ATTN_WIDTH + 2 * D_MODEL

LANES = 128
SUBLANES = 8
VMEM_LIMIT = 56 * 1024 * 1024

SSM_CHUNK = 16
SSM_ROW_BLOCK = 128
NEG_BIG = -1e30
MOBA_KV_CHUNK = 1024

_NT = (((1,), (1,)), ((), ()))


def _params(*sem):
    return pltpu.CompilerParams(dimension_semantics=sem, vmem_limit_bytes=VMEM_LIMIT)


def _dot_split(a, b, dims):
    bf, f32 = jnp.bfloat16, jnp.float32
    a_hi, b_hi = a.astype(bf), b.astype(bf)
    a_lo = (a - a_hi.astype(f32)).astype(bf)
    b_lo = (b - b_hi.astype(f32)).astype(bf)
    dot = lambda x, y: lax.dot_general(x, y, dims, preferred_element_type=f32)
    return dot(a_hi, b_hi) + (dot(a_hi, b_lo) + dot(a_lo, b_hi))


def _in_proj_kernel(x_ref, g_ref, w_ref, o_ref):
    x = x_ref[...]
    ms = jnp.mean(x * x, axis=-1, keepdims=True)
    h = (x * lax.rsqrt(ms + EPS) * g_ref[...]).astype(jnp.bfloat16)
    n_out = o_ref.shape[1]
    step = 512
    for c in range(n_out // step):
        o_ref[:, c * step:(c + 1) * step] = jnp.dot(
            h, w_ref[:, c * step:(c + 1) * step], preferred_element_type=jnp.float32)


def _in_proj(x2, g, w_bf, tm=256):
    T, D = x2.shape
    N = w_bf.shape[1]
    return pl.pallas_call(
        _in_proj_kernel,
        grid=(T // tm,),
        in_specs=[pl.BlockSpec((tm, D), lambda i: (i, 0)),
                  pl.BlockSpec((1, D), lambda i: (0, 0)),
                  pl.BlockSpec((D, N), lambda i: (0, 0))],
        out_specs=pl.BlockSpec((tm, N), lambda i: (i, 0)),
        out_shape=jax.ShapeDtypeStruct((T, N), jnp.float32),
        compiler_params=_params("parallel"),
        name="in_proj",
    )(x2, g.reshape(1, D), w_bf)


def _head_norm_rope(x, g, ones_bd, ct, sa, sb):
    bf, f32 = jnp.bfloat16, jnp.float32
    sq = x * x
    sq_hi = sq.astype(bf)
    sq_lo = (sq - sq_hi.astype(f32)).astype(bf)
    ssq = (jnp.dot(sq_hi, ones_bd, preferred_element_type=f32)
           + jnp.dot(sq_lo, ones_bd, preferred_element_type=f32))
    inv = lax.rsqrt(ssq * (1.0 / HEAD_DIM) + EPS)
    outs = []
    for t in range(x.shape[1] // LANES):
        cols = slice(t * LANES, (t + 1) * LANES)
        y = x[:, cols] * inv[:, cols] * g
        y = y * ct + pltpu.roll(y, LANES - ROPE_DIM // 2, 1) * sa + pltpu.roll(y, ROPE_DIM // 2, 1) * sb
        outs.append(y)
    return jnp.concatenate(outs, axis=1)


def _qk_prep_kernel(q_ref, k_ref, v_ref, gq_ref, gk_ref, ones_ref, ct_ref, sa_ref, sb_ref,
                    qo_ref, ko_ref, vo_ref, km_ref):
    ones_bd, ct, sa, sb = ones_ref[...], ct_ref[...], sa_ref[...], sb_ref[...]
    qn = _head_norm_rope(q_ref[...], gq_ref[...], ones_bd, ct, sa, sb)
    kn = _head_norm_rope(k_ref[...], gk_ref[...], ones_bd, ct, sa, sb)
    qo_ref[0] = qn
    ko_ref[0] = kn.astype(jnp.bfloat16)
    vo_ref[0] = v_ref[...].T.astype(jnp.bfloat16)
    km_ref[0, 0] = jnp.mean(kn, axis=0, keepdims=True)


def _rope_tables(S):
    half = ROPE_DIM // 2
    inv_freq = ROPE_THETA ** (-jnp.arange(half, dtype=jnp.float32) * 2.0 / ROPE_DIM)
    ang = jnp.arange(S, dtype=jnp.float32)[:, None] * inv_freq[None, :]
    cos, sin = jnp.cos(ang), jnp.sin(ang)
    ones = jnp.ones((S, HEAD_DIM - ROPE_DIM), jnp.float32)
    zeros = jnp.zeros((S, HEAD_DIM - ROPE_DIM), jnp.float32)
    zh = jnp.zeros((S, half), jnp.float32)
    ct = jnp.concatenate([cos, cos, ones], axis=1)
    sa = jnp.concatenate([-sin, zh, zeros], axis=1)
    sb = jnp.concatenate([zh, sin, zeros], axis=1)
    tile = lambda t: jnp.concatenate([t, t], axis=1)
    return tile(ct), tile(sa), tile(sb)


def _qk_prep(proj, gq, gk, B, S):
    ts = MOBA_BLOCK
    nb = S // ts
    W = ATTN_WIDTH
    ct, sa, sb = _rope_tables(S)
    head_of = jnp.arange(W) // HEAD_DIM
    ones_bd = (head_of[:, None] == head_of[None, :]).astype(jnp.bfloat16)
    g128 = lambda g: jnp.concatenate([g, g]).reshape(1, LANES)
    row = lambda b, s: b * nb + s
    tab = pl.BlockSpec((ts, LANES), lambda b, s: (s, 0))
    out3 = pl.BlockSpec((1, ts, W), lambda b, s: (b, s, 0))
    return pl.pallas_call(
        _qk_prep_kernel,
        grid=(B, nb),
        in_specs=[pl.BlockSpec((ts, W), lambda b, s: (row(b, s), 1)),
                  pl.BlockSpec((ts, W), lambda b, s: (row(b, s), 2)),
                  pl.BlockSpec((ts, W), lambda b, s: (row(b, s), 3)),
                  pl.BlockSpec((1, LANES), lambda b, s: (0, 0)),
                  pl.BlockSpec((1, LANES), lambda b, s: (0, 0)),
                  pl.BlockSpec((W, W), lambda b, s: (0, 0)),
                  tab, tab, tab],
        out_specs=[out3, out3, pl.BlockSpec((1, W, ts), lambda b, s: (b, 0, s)),
                   pl.BlockSpec((1, 1, 1, W), lambda b, s: (b, s, 0, 0))],
        out_shape=[jax.ShapeDtypeStruct((B, S, W), jnp.float32),
                   jax.ShapeDtypeStruct((B, S, W), jnp.bfloat16),
                   jax.ShapeDtypeStruct((B, W, S), jnp.bfloat16),
                   jax.ShapeDtypeStruct((B, nb, 1, W), jnp.float32)],
        compiler_params=_params("parallel", "parallel"),
        name="qk_prep",
    )(proj, proj, proj, g128(gq), g128(gk), ones_bd, ct, sa, sb)


def _moba_kernel(q_ref, k_ref, vt_ref, km_ref, oh_ref, o_ref, s_ref):
    i = pl.program_id(2)
    tq = q_ref.shape[1]
    nb = km_ref.shape[1]
    bf = jnp.bfloat16
    f32 = jnp.float32
    q = q_ref[0]
    km = km_ref[0]
    lane = lax.broadcasted_iota(jnp.int32, (tq, LANES), 1)
    blk = lax.broadcasted_iota(jnp.int32, (nb, tq), 0)
    past = blk < i
    scale = HEAD_DIM ** -0.5 * math.log2(math.e)

    lane_k = lax.broadcasted_iota(jnp.int32, (nb, LANES), 1)
    km2 = jnp.concatenate([jnp.where(lane_k < HEAD_DIM, km, 0.0),
                           jnp.where(lane_k < HEAD_DIM, 0.0, km)], axis=0)
    scores = _dot_split(km2, q, _NT)

    q_aug = []
    for h in range(2):
        hmask = (lane < HEAD_DIM) if h == 0 else (lane >= HEAD_DIM)
        qh = jnp.where(hmask, q, 0.0)
        s = jnp.where(past, scores[h * nb:(h + 1) * nb], -jnp.inf)
        sel = jnp.zeros((nb, tq), jnp.bool_)
        for _ in range(MOBA_TOPK):
            mx = jnp.max(s, axis=0, keepdims=True)
            first = jnp.min(jnp.where(s == mx, blk, nb), axis=0, keepdims=True)
            pick = blk == first
            sel = jnp.logical_or(sel, jnp.logical_and(pick, past))
            s = jnp.where(pick, -jnp.inf, s)
        bias_t = jnp.where(sel, 0.0, NEG_BIG)
        bias_t = jnp.concatenate([bias_t, jnp.zeros((LANES - nb, tq), f32)], axis=0)
        bias = bias_t.T.astype(bf)
        q_aug.append(jnp.concatenate([(qh * scale).astype(bf), bias], axis=1))

    ck = MOBA_KV_CHUNK
    n_chunks = lax.shift_right_logical(i + (ck // tq - 1), int(math.log2(ck // tq)))
    n_chunks = jnp.maximum(n_chunks, 1)

    def logits(c, slot):
        off = pl.multiple_of(c * ck, ck)
        k_aug = jnp.concatenate([k_ref[0, pl.ds(off, ck), :], oh_ref[pl.ds(off, ck), :]], axis=1)
        cm = []
        for h in range(2):
            s = lax.dot_general(k_aug, q_aug[h], _NT, preferred_element_type=f32)
            s_ref[slot, h] = s
            cm.append(jnp.max(s, axis=0, keepdims=True))
        return cm

    def weigh(s, vt, cm, m, l, acc):
        m_new = cm if m is None else jnp.maximum(m, cm)
        p = jnp.exp2(s - m_new)
        psum = jnp.sum(p, axis=0, keepdims=True)
        pv = jnp.dot(vt, p.astype(bf), preferred_element_type=f32)
        if m is None:
            return [m_new, psum, pv]
        alpha = jnp.exp2(m - m_new)
        return [m_new, alpha * l + psum, alpha * acc + pv]

    def weigh_chunk(c, slot, cm, state):
        off = pl.multiple_of(c * ck, ck)
        out = []
        for h in range(2):
            vt = vt_ref[0, h * HEAD_DIM:(h + 1) * HEAD_DIM, pl.ds(off, ck)]
            out += weigh(s_ref[slot, h], vt, cm[h], *state[3 * h:3 * h + 3])
        return out

    off_d = pl.multiple_of(i * tq, tq)
    kd = k_ref[0, pl.ds(off_d, tq), :]
    key_i = lax.broadcasted_iota(jnp.int32, (tq, tq), 0)
    qry_i = lax.broadcasted_iota(jnp.int32, (tq, tq), 1)
    causal = key_i <= qry_i
    state = []
    for h in range(2):
        s = lax.dot_general(kd, q_aug[h][:, :LANES], _NT, preferred_element_type=f32)
        s = jnp.where(causal, s, -jnp.inf)
        vt = vt_ref[0, h * HEAD_DIM:(h + 1) * HEAD_DIM, pl.ds(off_d, tq)]
        state += weigh(s, vt, jnp.max(s, axis=0, keepdims=True), None, None, None)

    cm0 = logits(0, 0)

    def body(c, carry):
        cm, state = carry[:2], carry[2:]
        state = weigh_chunk(c, c & 1, cm, state)
        cm_next = logits(c + 1, (c + 1) & 1)
        return tuple(cm_next) + tuple(state)

    carry = lax.fori_loop(0, n_chunks - 1, body, tuple(cm0) + tuple(state))
    last = n_chunks - 1
    _, l0, a0, _, l1, a1 = weigh_chunk(last, last & 1, carry[:2], carry[2:])
    o_t = jnp.concatenate([a0 / l0, a1 / l1], axis=0)
    o_ref[0] = o_t.T.astype(o_ref.dtype)


def _moba(q, k_bf, vt_bf, kmean):
    B, S, W = q.shape
    tq = MOBA_BLOCK
    nb = S // tq
    onehot = (jnp.arange(S)[:, None] // tq == jnp.arange(LANES)[None, :]).astype(jnp.bfloat16)
    qo = pl.BlockSpec((1, tq, LANES), lambda b, hp, i: (b, i, hp))
    return pl.pallas_call(
        _moba_kernel,
        grid=(B, W // LANES, nb),
        in_specs=[qo,
                  pl.BlockSpec((1, S, LANES), lambda b, hp, i: (b, 0, hp)),
                  pl.BlockSpec((1, LANES, S), lambda b, hp, i: (b, hp, 0)),
                  pl.BlockSpec((1, nb, LANES), lambda b, hp, i: (b, 0, hp)),
                  pl.BlockSpec((S, LANES), lambda b, hp, i: (0, 0))],
        out_specs=qo,
        out_shape=jax.ShapeDtypeStruct((B, S, W), jnp.bfloat16),
        scratch_shapes=[pltpu.VMEM((2, 2, MOBA_KV_CHUNK, tq), jnp.float32)],
        compiler_params=_params("parallel", "parallel", "arbitrary"),
        name="moba",
    )(q, k_bf, vt_bf, kmean, onehot)


def _ssm_weights(lam_re, lam_im, log_dt, b_re, b_im, c_re, c_im):
    L = SSM_CHUNK
    f32 = jnp.float32
    dt = jnp.exp(log_dt.astype(f32))[:, None]
    lr = jnp.minimum(lam_re.astype(f32), -1e-4)
    li = lam_im.astype(f32)
    mag = jnp.exp(lr * dt)
    ar = mag * jnp.cos(li * dt)
    ai = mag * jnp.sin(li * dt)
    den = lr * lr + li * li
    nr, ni = ar - 1.0, ai
    cr = (nr * lr + ni * li) / den
    ci = (ni * lr - nr * li) / den
    brf, bif = b_re.astype(f32), b_im.astype(f32)
    bbr = cr[..., None] * brf - ci[..., None] * bif
    bbi = cr[..., None] * bif + ci[..., None] * brf
    tau = jnp.arange(L + 1, dtype=f32)[:, None, None]
    pmag = jnp.exp(tau * (lr * dt)[None])
    pr = pmag * jnp.cos(tau * (li * dt)[None])
    pi = pmag * jnp.sin(tau * (li * dt)[None])
    cre, cim = c_re.astype(f32), c_im.astype(f32)
    hi = lax.Precision.HIGHEST
    car = cre[None] * pr[:, :, None, :] - cim[None] * pi[:, :, None, :]
    cai = cre[None] * pi[:, :, None, :] + cim[None] * pr[:, :, None, :]
    kern = (jnp.einsum('tgon,gni->tgoi', car[:L], bbr, precision=hi)
            - jnp.einsum('tgon,gni->tgoi', cai[:L], bbi, precision=hi))
    bf = jnp.bfloat16
    K = L * SSM_GROUP
    lag = jnp.arange(L)[:, None] - jnp.arange(L)[None, :]
    toep = jnp.where((lag >= 0)[:, :, None, None, None],
                     kern.astype(bf)[jnp.clip(lag, 0, L - 1)], 0)
    toep_t = toep.transpose(2, 0, 3, 1, 4).reshape(SSM_GROUPS, K, K)
    rev_r, rev_i = pr[L - 1::-1][:L], pi[L - 1::-1][:L]
    st_r = rev_r[:, :, :, None] * bbr[None] - rev_i[:, :, :, None] * bbi[None]
    st_i = rev_r[:, :, :, None] * bbi[None] + rev_i[:, :, :, None] * bbr[None]
    w_st = jnp.concatenate([st_r, st_i], axis=2)
    w_st_t = w_st.transpose(1, 2, 0, 3).reshape(SSM_GROUPS, 2 * SSM_STATE, K).astype(bf)
    w_out = jnp.concatenate([car[1:], -cai[1:]], axis=3)
    w_out_t = w_out.transpose(1, 0, 2, 3).reshape(SSM_GROUPS, K, 2 * SSM_STATE).astype(bf)
    a_chunk = jnp.concatenate([pr[L], pi[L]], axis=1)
    return toep_t, w_st_t, w_out_t, a_chunk


def _ssm_in_kernel(*refs):
    L, G, C = SSM_CHUNK, SSM_GROUPS, SSM_GROUP
    n_tiles = SSM_WIDTH // LANES
    u_refs, (w_ref, ut_ref, s_ref) = refs[:n_tiles], refs[n_tiles:]
    rb = ut_ref.shape[2]
    gq = LANES // C
    for q, u_ref in enumerate(u_refs):
        for s in range(L):
            xs = u_ref[pl.ds(s, rb, stride=L), :]
            ut_ref[q * gq:(q + 1) * gq, s * C:(s + 1) * C, :] = (
                xs.T.reshape(gq, C, rb).astype(ut_ref.dtype))

    for g in range(G):
        st = jnp.dot(w_ref[g], ut_ref[g], preferred_element_type=jnp.float32)
        s_ref[g] = st.T


def _ssm_scan_kernel(s_ref, p_ref, q_ref, o_ref):
    n_chunks = s_ref.shape[1]
    pm, qm = p_ref[...], q_ref[...]

    def body(c, carry):
        h, hs = carry
        o_ref[0, c] = h
        s = s_ref[0, c]
        return (h * pm + hs * qm + s, hs * pm - h * qm + pltpu.roll(s, SSM_STATE, 1))

    zero = jnp.zeros(pm.shape, jnp.float32)
    lax.fori_loop(0, n_chunks, body, (zero, zero), unroll=8)


def _ssm_out_kernel(ut_ref, h_ref, t_ref, w_ref, y_ref, yt_ref):
    L, G, C = SSM_CHUNK, SSM_GROUPS, SSM_GROUP
    rb = ut_ref.shape[2]

    for g in range(G):
        yt = jnp.dot(t_ref[g], ut_ref[g], preferred_element_type=jnp.float32)
        yt += lax.dot_general(w_ref[g], h_ref[g].astype(jnp.bfloat16), _NT,
                              preferred_element_type=jnp.float32)
        yt_ref[g] = yt
    gq = LANES // C
    for q in range(SSM_WIDTH // LANES):
        for t in range(L):
            z = yt_ref[q * gq:(q + 1) * gq, t * C:(t + 1) * C, :].reshape(LANES, rb)
            y_ref[q, pl.ds(t, rb, stride=L), :] = z.T


def _ssm(u_src, toep_t, w_st_t, w_out_t, a_chunk, B, S):
    L, G = SSM_CHUNK, SSM_GROUPS
    nc = S // L
    R = B * nc
    K = L * SSM_GROUP
    N2 = 2 * SSM_STATE
    n_tiles = SSM_WIDTH // LANES
    rb = min(SSM_ROW_BLOCK, R)
    bf = jnp.bfloat16
    ut, s_loc = pl.pallas_call(
        _ssm_in_kernel,
        grid=(R // rb,),
        in_specs=[pl.BlockSpec((rb * L, LANES), functools.partial(lambda q, j: (j, q), q))
                  for q in range(n_tiles)] + [pl.BlockSpec((G, N2, K), lambda j: (0, 0, 0))],
        out_specs=[pl.BlockSpec((G, K, rb), lambda j: (0, 0, j)),
                   pl.BlockSpec((G, rb, N2), lambda j: (0, j, 0))],
        out_shape=[jax.ShapeDtypeStruct((G, K, R), bf),
                   jax.ShapeDtypeStruct((G, R, N2), jnp.float32)],
        compiler_params=_params("parallel"),
        name="ssm_in",
    )(*([u_src] * n_tiles), w_st_t)
    s_bc = s_loc.reshape(G, B, nc, N2).transpose(1, 2, 0, 3)
    ar, ai = a_chunk[:, :SSM_STATE], a_chunk[:, SSM_STATE:]
    pm = jnp.concatenate([ar, ar], axis=1)
    qm = jnp.concatenate([-ai, ai], axis=1)
    h_in = pl.pallas_call(
        _ssm_scan_kernel,
        grid=(B,),
        in_specs=[pl.BlockSpec((1, nc, G, N2), lambda b: (b, 0, 0, 0)),
                  pl.BlockSpec((G, N2), lambda b: (0, 0)),
                  pl.BlockSpec((G, N2), lambda b: (0, 0))],
        out_specs=pl.BlockSpec((1, nc, G, N2), lambda b: (b, 0, 0, 0)),
        out_shape=jax.ShapeDtypeStruct((B, nc, G, N2), jnp.float32),
        compiler_params=_params("parallel"),
        name="ssm_scan",
    )(s_bc, pm, qm)
    h_g = h_in.transpose(2, 0, 1, 3).reshape(G, R, N2)
    return pl.pallas_call(
        _ssm_out_kernel,
        grid=(R // rb,),
        in_specs=[pl.BlockSpec((G, K, rb), lambda j: (0, 0, j)),
                  pl.BlockSpec((G, rb, N2), lambda j: (0, j, 0)),
                  pl.BlockSpec((G, K, K), lambda j: (0, 0, 0)),
                  pl.BlockSpec((G, K, N2), lambda j: (0, 0, 0))],
        out_specs=pl.BlockSpec((n_tiles, rb * L, LANES), lambda j: (0, j, 0)),
        out_shape=jax.ShapeDtypeStruct((n_tiles, B * S, LANES), jnp.float32),
        scratch_shapes=[pltpu.VMEM((G, K, rb), jnp.float32)],
        compiler_params=_params("parallel"),
        name="ssm_out",
    )(ut, h_g, toep_t, w_out_t)


def _route(logits):
    tm = logits.shape[0]
    lane = lax.broadcasted_iota(jnp.int32, (tm, LANES), 1)
    is_g = jnp.logical_and(lane >= N_EXPERTS, lane < N_EXPERTS + N_GROUPS)
    gl = jnp.where(is_g, logits, -jnp.inf)
    gmax = jnp.max(gl, axis=-1, keepdims=True)
    ge = jnp.exp(gl - gmax)
    gp = ge / jnp.sum(ge, axis=-1, keepdims=True)
    g_p = jnp.max(gp, axis=-1, keepdims=True)
    g_lane = jnp.min(jnp.where(jnp.logical_and(is_g, gp == g_p), lane, LANES), axis=-1, keepdims=True)
    g_idx = g_lane - N_EXPERTS
    grp_of_lane = lax.shift_right_logical(lane, int(math.log2(EXPERTS_PER_GROUP)))
    in_grp = jnp.logical_and(lane < N_EXPERTS, grp_of_lane == g_idx)
    el = jnp.where(in_grp, logits, -jnp.inf)
    v1 = jnp.max(el, axis=-1, keepdims=True)
    i1 = jnp.min(jnp.where(el == v1, lane, LANES), axis=-1, keepdims=True)
    el2 = jnp.where(lane == i1, -jnp.inf, el)
    v2 = jnp.max(el2, axis=-1, keepdims=True)
    i2 = jnp.min(jnp.where(el2 == v2, lane, LANES), axis=-1, keepdims=True)
    e2 = jnp.exp(v2 - v1)
    den = 1.0 + e2
    w1 = (1.0 / den) * g_p
    w2 = (e2 / den) * g_p
    return jnp.where(lane == i1, w1, jnp.where(lane == i2, w2, 0.0))


def _merge_kernel(y_ref, u_ref, att_ref, gs_ref, ga_ref, x_ref, d_ref, wglu_ref, bglu_ref,
                  wps_ref, wpa_ref, wo_ref, g2_ref, wr_ref, br_ref,
                  x1_ref, h2_ref, comb_ref):
    bf = jnp.bfloat16
    f32 = jnp.float32
    y_tiles = [y_ref[q] for q in range(y_ref.shape[0])]
    y = jnp.concatenate(y_tiles, axis=1) + d_ref[...] * u_ref[...]
    z = jax.nn.gelu(y)
    gate = jnp.dot(z.astype(bf), wglu_ref[...], preferred_element_type=f32) + bglu_ref[...]
    s5 = z * jax.nn.sigmoid(gate)
    y_ssm = jnp.dot(s5.astype(bf), wps_ref[...], preferred_element_type=f32)
    y_att = jnp.dot(att_ref[...], wpa_ref[...], preferred_element_type=f32)
    merged = jax.nn.sigmoid(gs_ref[...]) * y_ssm + jax.nn.sigmoid(ga_ref[...]) * y_att
    x1 = x_ref[...] + jnp.dot(merged.astype(bf), wo_ref[...], preferred_element_type=f32)
    x1_ref[...] = x1
    ms = jnp.mean(x1 * x1, axis=-1, keepdims=True)
    h2 = x1 * lax.rsqrt(ms + EPS) * g2_ref[...]
    h2_ref[...] = h2.astype(bf)
    logits = _dot_split(h2, wr_ref[...], (((1,), (0,)), ((), ()))) + br_ref[...]
    comb_ref[...] = _route(logits)


def _merge(y_raw, proj, att, x2, d, w_glu, b_glu, w_ps, w_pa, w_o, g2, w_r, b_r, tm=512):
    T, D = x2.shape
    W = SSM_WIDTH
    row = lambda shape, col: pl.BlockSpec(shape, lambda i: (i, col))
    full = lambda a: pl.BlockSpec(a.shape, lambda i: (0,) * a.ndim)
    consts = [d.reshape(1, W), w_glu, b_glu.reshape(1, W), w_ps, w_pa, w_o, g2.reshape(1, D), w_r, b_r]
    return pl.pallas_call(
        _merge_kernel,
        grid=(T // tm,),
        in_specs=[pl.BlockSpec((W // LANES, tm, LANES), lambda i: (0, i, 0)), row((tm, W), 0), row((tm, W), 0),
                  row((tm, D), 2), row((tm, D), 3), row((tm, D), 0)] + [full(c) for c in consts],
        out_specs=[row((tm, D), 0), row((tm, D), 0), row((tm, LANES), 0)],
        out_shape=[jax.ShapeDtypeStruct((T, D), jnp.float32),
                   jax.ShapeDtypeStruct((T, D), jnp.bfloat16),
                   jax.ShapeDtypeStruct((T, LANES), jnp.float32)],
        compiler_params=_params("parallel"),
        name="merge",
    )(y_raw, proj, att, proj, proj, x2, *consts)


def _moe_kernel(h_ref, comb_ref, x_ref, wg_ref, wu_ref, wd_ref, o_ref, act_ref):
    f32 = jnp.float32
    h = h_ref[...]
    comb = comb_ref[...]
    step = 4 * D_EXPERT
    for c in range(N_EXPERTS * D_EXPERT // step):
        hg = jnp.dot(h, wg_ref[:, c * step:(c + 1) * step], preferred_element_type=f32)
        hu = jnp.dot(h, wu_ref[:, c * step:(c + 1) * step], preferred_element_type=f32)
        a = jax.nn.silu(hg) * hu
        for j in range(step // D_EXPERT):
            e = c * (step // D_EXPERT) + j
            act_ref[:, e * D_EXPERT:(e + 1) * D_EXPERT] = (
                a[:, j * D_EXPERT:(j + 1) * D_EXPERT] * comb[:, e:e + 1]).astype(act_ref.dtype)
    o_ref[...] = x_ref[...] + jnp.dot(act_ref[...], wd_ref[...], preferred_element_type=f32)


def _moe(h2, comb, x1, wg, wu, wd, tm=512):
    T, D = x1.shape
    F = N_EXPERTS * D_EXPERT
    row = lambda w: pl.BlockSpec((tm, w), lambda i: (i, 0))
    once = lambda a: pl.BlockSpec(a.shape, lambda i: (0, 0), pipeline_mode=pl.Buffered(1))
    return pl.pallas_call(
        _moe_kernel,
        grid=(T // tm,),
        in_specs=[row(D), row(LANES), row(D), once(wg), once(wu), once(wd)],
        out_specs=row(D),
        out_shape=jax.ShapeDtypeStruct((T, D), jnp.float32),
        scratch_shapes=[pltpu.VMEM((tm, F), jnp.bfloat16)],
        compiler_params=_params("parallel"),
        name="moe",
    )(h2, comb, x1, wg, wu, wd)


def _layer(x, norm1_g, w_in, lam_re, lam_im, log_dt, ssm_b_re, ssm_b_im, ssm_c_re, ssm_c_im,
           ssm_d, w_glu, b_glu, q_norm_g, k_norm_g, w_proj_ssm, w_proj_attn, w_out, norm2_g,
           w_router_group, b_router_group, w_router_expert, b_router_expert, w_gate, w_up, w_down):
    B, S, D = x.shape
    bf = jnp.bfloat16
    x2 = x.reshape(B * S, D)
    proj = _in_proj(x2, norm1_g, w_in.astype(bf))
    q, k_bf, v_bf, kmean = _qk_prep(proj, q_norm_g, k_norm_g, B, S)
    att = _moba(q, k_bf, v_bf, kmean.reshape(B, S // MOBA_BLOCK, ATTN_WIDTH))
    toep, w_st, w_o_ssm, a_chunk = _ssm_weights(lam_re, lam_im, log_dt, ssm_b_re, ssm_b_im,
                                               ssm_c_re, ssm_c_im)
    y_raw = _ssm(proj, toep, w_st, w_o_ssm, a_chunk, B, S)
    pad = jnp.zeros((D, LANES - N_EXPERTS - N_GROUPS), jnp.float32)
    w_r = jnp.concatenate([w_router_expert, w_router_group, pad], axis=1)
    b_r = jnp.concatenate([b_router_expert, b_router_group,
                           jnp.zeros((LANES - N_EXPERTS - N_GROUPS,), jnp.float32)]).reshape(1, LANES)
    x1, h2, comb = _merge(y_raw, proj, att.reshape(B * S, ATTN_WIDTH), x2, ssm_d,
                          w_glu.astype(bf), b_glu, w_proj_ssm.astype(bf), w_proj_attn.astype(bf),
                          w_out.astype(bf), norm2_g, w_r, b_r)
    F = N_EXPERTS * D_EXPERT
    wg = w_gate.transpose(1, 0, 2).reshape(D, F).astype(bf)
    wu = w_up.transpose(1, 0, 2).reshape(D, F).astype(bf)
    wd = w_down.reshape(F, D).astype(bf)
    out = _moe(h2, comb, x1, wg, wu, wd)
    return out.reshape(B, S, D)


def kernel(x, norm1_g, w_in, lam_re, lam_im, log_dt, ssm_b_re, ssm_b_im, ssm_c_re, ssm_c_im,
           ssm_d, w_glu, b_glu, q_norm_g, k_norm_g, w_proj_ssm, w_proj_attn, w_out, norm2_g,
           w_router_group, b_router_group, w_router_expert, b_router_expert, w_gate, w_up, w_down):
    args = (norm1_g, w_in, lam_re, lam_im, log_dt, ssm_b_re, ssm_b_im, ssm_c_re, ssm_c_im,
            ssm_d, w_glu, b_glu, q_norm_g, k_norm_g, w_proj_ssm, w_proj_attn, w_out, norm2_g,
            w_router_group, b_router_group, w_router_expert, b_router_expert, w_gate, w_up, w_down)
    for l in range(w_in.shape[0]):
        x = _layer(x, *(a[l] for a in args))
    return x
```

```python
import functools
import math

import jax
import jax.numpy as jnp
from jax import lax
from jax.experimental import pallas as pl
from jax.experimental.pallas import tpu as pltpu

D_MODEL = 1024
SSM_WIDTH = 512
SSM_GROUP = 16
SSM_GROUPS = SSM_WIDTH // SSM_GROUP
SSM_STATE = 64
N_HEADS = 8
HEAD_DIM = 64
ATTN_WIDTH = N_HEADS * HEAD_DIM
MOBA_BLOCK = 256
MOBA_TOPK = 3
ROPE_THETA = 500000.0
ROPE_DIM = HEAD_DIM // 4
N_GROUPS = 4
EXPERTS_PER_GROUP = 8
N_EXPERTS = N_GROUPS * EXPERTS_PER_GROUP
D_EXPERT = 128
EPS = 1e-6
IN_WIDTH = SSM_WIDTH + 3 * ATTN_WIDTH + 2 * D_MODEL

LANES = 128
SUBLANES = 8
VMEM_LIMIT = 56 * 1024 * 1024

SSM_CHUNK = 16
SSM_ROW_BLOCK = 128
NEG_BIG = -1e30
MOBA_KV_CHUNK = 1024

_NT = (((1,), (1,)), ((), ()))


def _params(*sem):
    return pltpu.CompilerParams(dimension_semantics=sem, vmem_limit_bytes=VMEM_LIMIT)


def _dot_split(a, b, dims):
    bf, f32 = jnp.bfloat16, jnp.float32
    a_hi, b_hi = a.astype(bf), b.astype(bf)
    a_lo = (a - a_hi.astype(f32)).astype(bf)
    b_lo = (b - b_hi.astype(f32)).astype(bf)
    dot = lambda x, y: lax.dot_general(x, y, dims, preferred_element_type=f32)
    return dot(a_hi, b_hi) + (dot(a_hi, b_lo) + dot(a_lo, b_hi))


def _head_norm_rope(x, g, ones_bd, ct, sa, sb):
    bf, f32 = jnp.bfloat16, jnp.float32
    sq = x * x
    sq_hi = sq.astype(bf)
    sq_lo = (sq - sq_hi.astype(f32)).astype(bf)
    ssq = (jnp.dot(sq_hi, ones_bd, preferred_element_type=f32)
           + jnp.dot(sq_lo, ones_bd, preferred_element_type=f32))
    inv = lax.rsqrt(ssq * (1.0 / HEAD_DIM) + EPS)
    outs = []
    for t in range(x.shape[1] // LANES):
        cols = slice(t * LANES, (t + 1) * LANES)
        y = x[:, cols] * inv[:, cols] * g
        y = y * ct + pltpu.roll(y, LANES - ROPE_DIM // 2, 1) * sa + pltpu.roll(y, ROPE_DIM // 2, 1) * sb
        outs.append(y)
    return jnp.concatenate(outs, axis=1)


def _rope_tables(S):
    half = ROPE_DIM // 2
    inv_freq = ROPE_THETA ** (-jnp.arange(half, dtype=jnp.float32) * 2.0 / ROPE_DIM)
    ang = jnp.arange(S, dtype=jnp.float32)[:, None] * inv_freq[None, :]
    cos, sin = jnp.cos(ang), jnp.sin(ang)
    ones = jnp.ones((S, HEAD_DIM - ROPE_DIM), jnp.float32)
    zeros = jnp.zeros((S, HEAD_DIM - ROPE_DIM), jnp.float32)
    zh = jnp.zeros((S, half), jnp.float32)
    ct = jnp.concatenate([cos, cos, ones], axis=1)
    sa = jnp.concatenate([-sin, zh, zeros], axis=1)
    sb = jnp.concatenate([zh, sin, zeros], axis=1)
    tile = lambda t: jnp.concatenate([t, t], axis=1)
    return tile(ct), tile(sa), tile(sb)


def _in_proj_kernel(x_ref, g_ref, w_ref, gq_ref, gk_ref, ones_ref, ct_ref, sa_ref, sb_ref,
                    gu_ref, qo_ref, ko_ref, vo_ref, km_ref):
    bf = jnp.bfloat16
    x = x_ref[...]
    ms = jnp.mean(x * x, axis=-1, keepdims=True)
    h = (x * lax.rsqrt(ms + EPS) * g_ref[...]).astype(bf)
    proj = lambda lo, hi: jnp.dot(h, w_ref[:, lo:hi], preferred_element_type=jnp.float32)
    q0 = SSM_WIDTH
    k0, v0, g0 = q0 + ATTN_WIDTH, q0 + 2 * ATTN_WIDTH, q0 + 3 * ATTN_WIDTH
    step = 512
    n_gate = (IN_WIDTH - g0) // step
    for c in range(n_gate):
        gu_ref[:, c * step:(c + 1) * step] = proj(g0 + c * step, g0 + (c + 1) * step)
    gu_ref[:, n_gate * step:] = proj(0, q0)
    ones_bd, ct, sa, sb = ones_ref[...], ct_ref[...], sa_ref[...], sb_ref[...]
    qo_ref[0] = _head_norm_rope(proj(q0, k0), gq_ref[...], ones_bd, ct, sa, sb)
    kn = _head_norm_rope(proj(k0, v0), gk_ref[...], ones_bd, ct, sa, sb)
    ko_ref[0] = kn.astype(bf)
    km_ref[0, 0] = jnp.mean(kn, axis=0, keepdims=True)
    vo_ref[0] = proj(v0, g0).T.astype(bf)


def _in_proj(x2, g, w_bf, gq, gk, B, S):
    ts = MOBA_BLOCK
    nb = S // ts
    T, D = x2.shape
    W = ATTN_WIDTH
    n_gu = IN_WIDTH - 3 * W
    ct, sa, sb = _rope_tables(S)
    head_of = jnp.arange(W) // HEAD_DIM
    ones_bd = (head_of[:, None] == head_of[None, :]).astype(jnp.bfloat16)
    g128 = lambda t: jnp.concatenate([t, t]).reshape(1, LANES)
    const = lambda shape: pl.BlockSpec(shape, lambda b, s: (0,) * len(shape))
    tab = pl.BlockSpec((ts, LANES), lambda b, s: (s, 0))
    tok3 = pl.BlockSpec((1, ts, W), lambda b, s: (b, s, 0))
    return pl.pallas_call(
        _in_proj_kernel,
        grid=(B, nb),
        in_specs=[pl.BlockSpec((ts, D), lambda b, s: (b * nb + s, 0)),
                  const((1, D)), const(w_bf.shape), const((1, LANES)), const((1, LANES)),
                  const((W, W)), tab, tab, tab],
        out_specs=[pl.BlockSpec((ts, n_gu), lambda b, s: (b * nb + s, 0)), tok3, tok3,
                   pl.BlockSpec((1, W, ts), lambda b, s: (b, 0, s)),
                   pl.BlockSpec((1, 1, 1, W), lambda b, s: (b, s, 0, 0))],
        out_shape=[jax.ShapeDtypeStruct((T, n_gu), jnp.float32),
                   jax.ShapeDtypeStruct((B, S, W), jnp.float32),
                   jax.ShapeDtypeStruct((B, S, W), jnp.bfloat16),
                   jax.ShapeDtypeStruct((B, W, S), jnp.bfloat16),
                   jax.ShapeDtypeStruct((B, nb, 1, W), jnp.float32)],
        compiler_params=_params("parallel", "parallel"),
        name="in_proj",
    )(x2, g.reshape(1, D), w_bf, g128(gq), g128(gk), ones_bd, ct, sa, sb)


def _moba_kernel(q_ref, k_ref, vt_ref, km_ref, oh_ref, o_ref, s_ref):
    i = pl.program_id(2)
    tq = q_ref.shape[1]
    nb = km_ref.shape[1]
    bf = jnp.bfloat16
    f32 = jnp.float32
    q = q_ref[0]
    km = km_ref[0]
    lane = lax.broadcasted_iota(jnp.int32, (tq, LANES), 1)
    blk = lax.broadcasted_iota(jnp.int32, (nb, tq), 0)
    past = blk < i
    scale = HEAD_DIM ** -0.5 * math.log2(math.e)

    lane_k = lax.broadcasted_iota(jnp.int32, (nb, LANES), 1)
    km2 = jnp.concatenate([jnp.where(lane_k < HEAD_DIM, km, 0.0),
                           jnp.where(lane_k < HEAD_DIM, 0.0, km)], axis=0)
    scores = _dot_split(km2, q, _NT)

    q_aug = []
    for h in range(2):
        hmask = (lane < HEAD_DIM) if h == 0 else (lane >= HEAD_DIM)
        qh = jnp.where(hmask, q, 0.0)
        s = jnp.where(past, scores[h * nb:(h + 1) * nb], -jnp.inf)
        sel = jnp.zeros((nb, tq), jnp.bool_)
        for _ in range(MOBA_TOPK):
            mx = jnp.max(s, axis=0, keepdims=True)
            first = jnp.min(jnp.where(s == mx, blk, nb), axis=0, keepdims=True)
            pick = blk == first
            sel = jnp.logical_or(sel, jnp.logical_and(pick, past))
            s = jnp.where(pick, -jnp.inf, s)
        bias_t = jnp.where(sel, 0.0, NEG_BIG)
        bias_t = jnp.concatenate([bias_t, jnp.zeros((LANES - nb, tq), f32)], axis=0)
        bias = bias_t.T.astype(bf)
        q_aug.append(jnp.concatenate([(qh * scale).astype(bf), bias], axis=1))

    ck = MOBA_KV_CHUNK
    n_chunks = lax.shift_right_logical(i + (ck // tq - 1), int(math.log2(ck // tq)))
    n_chunks = jnp.maximum(n_chunks, 1)

    def logits(c, slot):
        off = pl.multiple_of(c * ck, ck)
        k_aug = jnp.concatenate([k_ref[0, pl.ds(off, ck), :], oh_ref[pl.ds(off, ck), :]], axis=1)
        cm = []
        for h in range(2):
            s = lax.dot_general(k_aug, q_aug[h], _NT, preferred_element_type=f32)
            s_ref[slot, h] = s
            cm.append(jnp.max(s, axis=0, keepdims=True))
        return cm

    def weigh(s, vt, cm, m, l, acc):
        m_new = cm if m is None else jnp.maximum(m, cm)
        p = jnp.exp2(s - m_new)
        psum = jnp.sum(p, axis=0, keepdims=True)
        pv = jnp.dot(vt, p.astype(bf), preferred_element_type=f32)
        if m is None:
            return [m_new, psum, pv]
        alpha = jnp.exp2(m - m_new)
        return [m_new, alpha * l + psum, alpha * acc + pv]

    def weigh_chunk(c, slot, cm, state):
        off = pl.multiple_of(c * ck, ck)
        out = []
        for h in range(2):
            vt = vt_ref[0, h * HEAD_DIM:(h + 1) * HEAD_DIM, pl.ds(off, ck)]
            out += weigh(s_ref[slot, h], vt, cm[h], *state[3 * h:3 * h + 3])
        return out

    off_d = pl.multiple_of(i * tq, tq)
    kd = k_ref[0, pl.ds(off_d, tq), :]
    key_i = lax.broadcasted_iota(jnp.int32, (tq, tq), 0)
    qry_i = lax.broadcasted_iota(jnp.int32, (tq, tq), 1)
    causal = key_i <= qry_i
    state = []
    for h in range(2):
        s = lax.dot_general(kd, q_aug[h][:, :LANES], _NT, preferred_element_type=f32)
        s = jnp.where(causal, s, -jnp.inf)
        vt = vt_ref[0, h * HEAD_DIM:(h + 1) * HEAD_DIM, pl.ds(off_d, tq)]
        state += weigh(s, vt, jnp.max(s, axis=0, keepdims=True), None, None, None)

    cm0 = logits(0, 0)

    def body(c, carry):
        cm, state = carry[:2], carry[2:]
        state = weigh_chunk(c, c & 1, cm, state)
        cm_next = logits(c + 1, (c + 1) & 1)
        return tuple(cm_next) + tuple(state)

    carry = lax.fori_loop(0, n_chunks - 1, body, tuple(cm0) + tuple(state))
    last = n_chunks - 1
    _, l0, a0, _, l1, a1 = weigh_chunk(last, last & 1, carry[:2], carry[2:])
    o_t = jnp.concatenate([a0 / l0, a1 / l1], axis=0)
    o_ref[0] = o_t.T.astype(o_ref.dtype)


def _moba(q, k_bf, vt_bf, kmean):
    B, S, W = q.shape
    tq = MOBA_BLOCK
    nb = S // tq
    onehot = (jnp.arange(S)[:, None] // tq == jnp.arange(LANES)[None, :]).astype(jnp.bfloat16)
    qo = pl.BlockSpec((1, tq, LANES), lambda b, hp, i: (b, i, hp))
    return pl.pallas_call(
        _moba_kernel,
        grid=(B, W // LANES, nb),
        in_specs=[qo,
                  pl.BlockSpec((1, S, LANES), lambda b, hp, i: (b, 0, hp)),
                  pl.BlockSpec((1, LANES, S), lambda b, hp, i: (b, hp, 0)),
                  pl.BlockSpec((1, nb, LANES), lambda b, hp, i: (b, 0, hp)),
                  pl.BlockSpec((S, LANES), lambda b, hp, i: (0, 0))],
        out_specs=qo,
        out_shape=jax.ShapeDtypeStruct((B, S, W), jnp.bfloat16),
        scratch_shapes=[pltpu.VMEM((2, 2, MOBA_KV_CHUNK, tq), jnp.float32)],
        compiler_params=_params("parallel", "parallel", "arbitrary"),
        name="moba",
    )(q, k_bf, vt_bf, kmean, onehot)


def _ssm_weights(lam_re, lam_im, log_dt, b_re, b_im, c_re, c_im):
    L = SSM_CHUNK
    f32 = jnp.float32
    dt = jnp.exp(log_dt.astype(f32))[:, None]
    lr = jnp.minimum(lam_re.astype(f32), -1e-4)
    li = lam_im.astype(f32)
    mag = jnp.exp(lr * dt)
    ar = mag * jnp.cos(li * dt)
    ai = mag * jnp.sin(li * dt)
    den = lr * lr + li * li
    nr, ni = ar - 1.0, ai
    cr = (nr * lr + ni * li) / den
    ci = (ni * lr - nr * li) / den
    brf, bif = b_re.astype(f32), b_im.astype(f32)
    bbr = cr[..., None] * brf - ci[..., None] * bif
    bbi = cr[..., None] * bif + ci[..., None] * brf
    tau = jnp.arange(L + 1, dtype=f32)[:, None, None]
    pmag = jnp.exp(tau * (lr * dt)[None])
    pr = pmag * jnp.cos(tau * (li * dt)[None])
    pi = pmag * jnp.sin(tau * (li * dt)[None])
    cre, cim = c_re.astype(f32), c_im.astype(f32)
    hi = lax.Precision.HIGHEST
    car = cre[None] * pr[:, :, None, :] - cim[None] * pi[:, :, None, :]
    cai = cre[None] * pi[:, :, None, :] + cim[None] * pr[:, :, None, :]
    kern = (jnp.einsum('tgon,gni->tgoi', car[:L], bbr, precision=hi)
            - jnp.einsum('tgon,gni->tgoi', cai[:L], bbi, precision=hi))
    bf = jnp.bfloat16
    K = L * SSM_GROUP
    lag = jnp.arange(L)[:, None] - jnp.arange(L)[None, :]
    toep = jnp.where((lag >= 0)[:, :, None, None, None],
                     kern.astype(bf)[jnp.clip(lag, 0, L - 1)], 0)
    toep_t = toep.transpose(2, 0, 3, 1, 4).reshape(SSM_GROUPS, K, K)
    rev_r, rev_i = pr[L - 1::-1][:L], pi[L - 1::-1][:L]
    st_r = rev_r[:, :, :, None] * bbr[None] - rev_i[:, :, :, None] * bbi[None]
    st_i = rev_r[:, :, :, None] * bbi[None] + rev_i[:, :, :, None] * bbr[None]
    w_st = jnp.concatenate([st_r, st_i], axis=2)
    w_st_t = w_st.transpose(1, 2, 0, 3).reshape(SSM_GROUPS, 2 * SSM_STATE, K).astype(bf)
    w_out = jnp.concatenate([car[1:], -cai[1:]], axis=3)
    w_out_t = w_out.transpose(1, 0, 2, 3).reshape(SSM_GROUPS, K, 2 * SSM_STATE).astype(bf)
    a_chunk = jnp.concatenate([pr[L], pi[L]], axis=1)
    return toep_t, w_st_t, w_out_t, a_chunk


def _ssm_in_kernel(*refs):
    L, G, C = SSM_CHUNK, SSM_GROUPS, SSM_GROUP
    n_tiles = SSM_WIDTH // LANES
    u_refs, (w_ref, ut_ref, s_ref) = refs[:n_tiles], refs[n_tiles:]
    rb = ut_ref.shape[2]
    gq = LANES // C
    for q, u_ref in enumerate(u_refs):
        for s in range(L):
            xs = u_ref[pl.ds(s, rb, stride=L), :]
            ut_ref[q * gq:(q + 1) * gq, s * C:(s + 1) * C, :] = (
                xs.T.reshape(gq, C, rb).astype(ut_ref.dtype))

    for g in range(G):
        st = jnp.dot(w_ref[g], ut_ref[g], preferred_element_type=jnp.float32)
        s_ref[g] = st.T


def _ssm_scan_kernel(s_ref, p_ref, q_ref, o_ref):
    n_chunks = s_ref.shape[1]
    pm, qm = p_ref[...], q_ref[...]

    def body(c, carry):
        h, hs = carry
        o_ref[0, c] = h
        s = s_ref[0, c]
        return (h * pm + hs * qm + s, hs * pm - h * qm + pltpu.roll(s, SSM_STATE, 1))

    zero = jnp.zeros(pm.shape, jnp.float32)
    lax.fori_loop(0, n_chunks, body, (zero, zero), unroll=8)


def _ssm_out_kernel(ut_ref, h_ref, t_ref, w_ref, y_ref, yt_ref):
    L, G, C = SSM_CHUNK, SSM_GROUPS, SSM_GROUP
    rb = ut_ref.shape[2]

    for g in range(G):
        yt = jnp.dot(t_ref[g], ut_ref[g], preferred_element_type=jnp.float32)
        yt += lax.dot_general(w_ref[g], h_ref[g].astype(jnp.bfloat16), _NT,
                              preferred_element_type=jnp.float32)
        yt_ref[g] = yt
    gq = LANES // C
    for q in range(SSM_WIDTH // LANES):
        for t in range(L):
            z = yt_ref[q * gq:(q + 1) * gq, t * C:(t + 1) * C, :].reshape(LANES, rb)
            y_ref[q, pl.ds(t, rb, stride=L), :] = z.T


def _ssm(u_src, toep_t, w_st_t, w_out_t, a_chunk, B, S, u_tile0=0):
    L, G = SSM_CHUNK, SSM_GROUPS
    nc = S // L
    R = B * nc
    K = L * SSM_GROUP
    N2 = 2 * SSM_STATE
    n_tiles = SSM_WIDTH // LANES
    rb = min(SSM_ROW_BLOCK, R)
    bf = jnp.bfloat16
    ut, s_loc = pl.pallas_call(
        _ssm_in_kernel,
        grid=(R // rb,),
        in_specs=[pl.BlockSpec((rb * L, LANES), functools.partial(lambda q, j: (j, q), u_tile0 + q))
                  for q in range(n_tiles)] + [pl.BlockSpec((G, N2, K), lambda j: (0, 0, 0))],
        out_specs=[pl.BlockSpec((G, K, rb), lambda j: (0, 0, j)),
                   pl.BlockSpec((G, rb, N2), lambda j: (0, j, 0))],
        out_shape=[jax.ShapeDtypeStruct((G, K, R), bf),
                   jax.ShapeDtypeStruct((G, R, N2), jnp.float32)],
        compiler_params=_params("parallel"),
        name="ssm_in",
    )(*([u_src] * n_tiles), w_st_t)
    s_bc = s_loc.reshape(G, B, nc, N2).transpose(1, 2, 0, 3)
    ar, ai = a_chunk[:, :SSM_STATE], a_chunk[:, SSM_STATE:]
    pm = jnp.concatenate([ar, ar], axis=1)
    qm = jnp.concatenate([-ai, ai], axis=1)
    h_in = pl.pallas_call(
        _ssm_scan_kernel,
        grid=(B,),
        in_specs=[pl.BlockSpec((1, nc, G, N2), lambda b: (b, 0, 0, 0)),
                  pl.BlockSpec((G, N2), lambda b: (0, 0)),
                  pl.BlockSpec((G, N2), lambda b: (0, 0))],
        out_specs=pl.BlockSpec((1, nc, G, N2), lambda b: (b, 0, 0, 0)),
        out_shape=jax.ShapeDtypeStruct((B, nc, G, N2), jnp.float32),
        compiler_params=_params("parallel"),
        name="ssm_scan",
    )(s_bc, pm, qm)
    h_g = h_in.transpose(2, 0, 1, 3).reshape(G, R, N2)
    return pl.pallas_call(
        _ssm_out_kernel,
        grid=(R // rb,),
        in_specs=[pl.BlockSpec((G, K, rb), lambda j: (0, 0, j)),
                  pl.BlockSpec((G, rb, N2), lambda j: (0, j, 0)),
                  pl.BlockSpec((G, K, K), lambda j: (0, 0, 0)),
                  pl.BlockSpec((G, K, N2), lambda j: (0, 0, 0))],
        out_specs=pl.BlockSpec((n_tiles, rb * L, LANES), lambda j: (0, j, 0)),
        out_shape=jax.ShapeDtypeStruct((n_tiles, B * S, LANES), jnp.float32),
        scratch_shapes=[pltpu.VMEM((G, K, rb), jnp.float32)],
        compiler_params=_params("parallel"),
        name="ssm_out",
    )(ut, h_g, toep_t, w_out_t)


def _route(logits):
    tm = logits.shape[0]
    lane = lax.broadcasted_iota(jnp.int32, (tm, LANES), 1)
    is_g = jnp.logical_and(lane >= N_EXPERTS, lane < N_EXPERTS + N_GROUPS)
    gl = jnp.where(is_g, logits, -jnp.inf)
    gmax = jnp.max(gl, axis=-1, keepdims=True)
    ge = jnp.exp(gl - gmax)
    gp = ge / jnp.sum(ge, axis=-1, keepdims=True)
    g_p = jnp.max(gp, axis=-1, keepdims=True)
    g_lane = jnp.min(jnp.where(jnp.logical_and(is_g, gp == g_p), lane, LANES), axis=-1, keepdims=True)
    g_idx = g_lane - N_EXPERTS
    grp_of_lane = lax.shift_right_logical(lane, int(math.log2(EXPERTS_PER_GROUP)))
    in_grp = jnp.logical_and(lane < N_EXPERTS, grp_of_lane == g_idx)
    el = jnp.where(in_grp, logits, -jnp.inf)
    v1 = jnp.max(el, axis=-1, keepdims=True)
    i1 = jnp.min(jnp.where(el == v1, lane, LANES), axis=-1, keepdims=True)
    el2 = jnp.where(lane == i1, -jnp.inf, el)
    v2 = jnp.max(el2, axis=-1, keepdims=True)
    i2 = jnp.min(jnp.where(el2 == v2, lane, LANES), axis=-1, keepdims=True)
    e2 = jnp.exp(v2 - v1)
    den = 1.0 + e2
    w1 = (1.0 / den) * g_p
    w2 = (e2 / den) * g_p
    return jnp.where(lane == i1, w1, jnp.where(lane == i2, w2, 0.0))


def _merge_kernel(y_ref, u_ref, att_ref, gs_ref, ga_ref, x_ref, d_ref, wglu_ref, bglu_ref,
                  wps_ref, wpa_ref, wo_ref, g2_ref, wr_ref, br_ref,
                  x1_ref, h2_ref, comb_ref):
    bf = jnp.bfloat16
    f32 = jnp.float32
    y_tiles = [y_ref[q] for q in range(y_ref.shape[0])]
    y = jnp.concatenate(y_tiles, axis=1) + d_ref[...] * u_ref[...]
    z = jax.nn.gelu(y)
    gate = jnp.dot(z.astype(bf), wglu_ref[...], preferred_element_type=f32) + bglu_ref[...]
    s5 = z * jax.nn.sigmoid(gate)
    y_ssm = jnp.dot(s5.astype(bf), wps_ref[...], preferred_element_type=f32)
    y_att = jnp.dot(att_ref[...], wpa_ref[...], preferred_element_type=f32)
    merged = jax.nn.sigmoid(gs_ref[...]) * y_ssm + jax.nn.sigmoid(ga_ref[...]) * y_att
    x1 = x_ref[...] + jnp.dot(merged.astype(bf), wo_ref[...], preferred_element_type=f32)
    x1_ref[...] = x1
    ms = jnp.mean(x1 * x1, axis=-1, keepdims=True)
    h2 = x1 * lax.rsqrt(ms + EPS) * g2_ref[...]
    h2_ref[...] = h2.astype(bf)
    logits = _dot_split(h2, wr_ref[...], (((1,), (0,)), ((), ()))) + br_ref[...]
    comb_ref[...] = _route(logits)


def _merge(y_raw, gu, att, x2, d, w_glu, b_glu, w_ps, w_pa, w_o, g2, w_r, b_r, tm=512):
    T, D = x2.shape
    W = SSM_WIDTH
    row = lambda shape, col: pl.BlockSpec(shape, lambda i: (i, col))
    full = lambda a: pl.BlockSpec(a.shape, lambda i: (0,) * a.ndim)
    consts = [d.reshape(1, W), w_glu, b_glu.reshape(1, W), w_ps, w_pa, w_o, g2.reshape(1, D), w_r, b_r]
    return pl.pallas_call(
        _merge_kernel,
        grid=(T // tm,),
        in_specs=[pl.BlockSpec((W // LANES, tm, LANES), lambda i: (0, i, 0)), row((tm, W), 2 * D // W), row((tm, W), 0),
                  row((tm, D), 0), row((tm, D), 1), row((tm, D), 0)] + [full(c) for c in consts],
        out_specs=[row((tm, D), 0), row((tm, D), 0), row((tm, LANES), 0)],
        out_shape=[jax.ShapeDtypeStruct((T, D), jnp.float32),
                   jax.ShapeDtypeStruct((T, D), jnp.bfloat16),
                   jax.ShapeDtypeStruct((T, LANES), jnp.float32)],
        compiler_params=_params("parallel"),
        name="merge",
    )(y_raw, gu, att, gu, gu, x2, *consts)


def _moe_kernel(h_ref, comb_ref, x_ref, wg_ref, wu_ref, wd_ref, o_ref, act_ref):
    f32 = jnp.float32
    h = h_ref[...]
    comb = comb_ref[...]
    step = 4 * D_EXPERT
    for c in range(N_EXPERTS * D_EXPERT // step):
        hg = jnp.dot(h, wg_ref[:, c * step:(c + 1) * step], preferred_element_type=f32)
        hu = jnp.dot(h, wu_ref[:, c * step:(c + 1) * step], preferred_element_type=f32)
        a = jax.nn.silu(hg) * hu
        for j in range(step // D_EXPERT):
            e = c * (step // D_EXPERT) + j
            act_ref[:, e * D_EXPERT:(e + 1) * D_EXPERT] = (
                a[:, j * D_EXPERT:(j + 1) * D_EXPERT] * comb[:, e:e + 1]).astype(act_ref.dtype)
    o_ref[...] = x_ref[...] + jnp.dot(act_ref[...], wd_ref[...], preferred_element_type=f32)


def _moe(h2, comb, x1, wg, wu, wd, tm=512):
    T, D = x1.shape
    F = N_EXPERTS * D_EXPERT
    row = lambda w: pl.BlockSpec((tm, w), lambda i: (i, 0))
    once = lambda a: pl.BlockSpec(a.shape, lambda i: (0, 0), pipeline_mode=pl.Buffered(1))
    return pl.pallas_call(
        _moe_kernel,
        grid=(T // tm,),
        in_specs=[row(D), row(LANES), row(D), once(wg), once(wu), once(wd)],
        out_specs=row(D),
        out_shape=jax.ShapeDtypeStruct((T, D), jnp.float32),
        scratch_shapes=[pltpu.VMEM((tm, F), jnp.bfloat16)],
        compiler_params=_params("parallel"),
        name="moe",
    )(h2, comb, x1, wg, wu, wd)


def _layer(x, norm1_g, w_in, lam_re, lam_im, log_dt, ssm_b_re, ssm_b_im, ssm_c_re, ssm_c_im,
           ssm_d, w_glu, b_glu, q_norm_g, k_norm_g, w_proj_ssm, w_proj_attn, w_out, norm2_g,
           w_router_group, b_router_group, w_router_expert, b_router_expert, w_gate, w_up, w_down):
    B, S, D = x.shape
    bf = jnp.bfloat16
    x2 = x.reshape(B * S, D)
    gu, q, k_bf, v_bf, kmean = _in_proj(x2, norm1_g, w_in.astype(bf), q_norm_g, k_norm_g, B, S)
    att = _moba(q, k_bf, v_bf, kmean.reshape(B, S // MOBA_BLOCK, ATTN_WIDTH))
    toep, w_st, w_o_ssm, a_chunk = _ssm_weights(lam_re, lam_im, log_dt, ssm_b_re, ssm_b_im,
                                               ssm_c_re, ssm_c_im)
    y_raw = _ssm(gu, toep, w_st, w_o_ssm, a_chunk, B, S, u_tile0=2 * D // LANES)
    pad = jnp.zeros((D, LANES - N_EXPERTS - N_GROUPS), jnp.float32)
    w_r = jnp.concatenate([w_router_expert, w_router_group, pad], axis=1)
    b_r = jnp.concatenate([b_router_expert, b_router_group,
                           jnp.zeros((LANES - N_EXPERTS - N_GROUPS,), jnp.float32)]).reshape(1, LANES)
    x1, h2, comb = _merge(y_raw, gu, att.reshape(B * S, ATTN_WIDTH), x2, ssm_d,
                          w_glu.astype(bf), b_glu, w_proj_ssm.astype(bf), w_proj_attn.astype(bf),
                          w_out.astype(bf), norm2_g, w_r, b_r)
    F = N_EXPERTS * D_EXPERT
    wg = w_gate.transpose(1, 0, 2).reshape(D, F).astype(bf)
    wu = w_up.transpose(1, 0, 2).reshape(D, F).astype(bf)
    wd = w_down.reshape(F, D).astype(bf)
    out = _moe(h2, comb, x1, wg, wu, wd)
    return out.reshape(B, S, D)


def kernel(x, norm1_g, w_in, lam_re, lam_im, log_dt, ssm_b_re, ssm_b_im, ssm_c_re, ssm_c_im,
           ssm_d, w_glu, b_glu, q_norm_g, k_norm_g, w_proj_ssm, w_proj_attn, w_out, norm2_g,
           w_router_group, b_router_group, w_router_expert, b_router_expert, w_gate, w_up, w_down):
    args = (norm1_g, w_in, lam_re, lam_im, log_dt, ssm_b_re, ssm_b_im, ssm_c_re, ssm_c_im,
            ssm_d, w_glu, b_glu, q_norm_g, k_norm_g, w_proj_ssm, w_proj_attn, w_out, norm2_g,
            w_router_group, b_router_group, w_router_expert, b_router_expert, w_gate, w_up, w_down)
    for l in range(w_in.shape[0]):
        x = _layer(x, *(a[l] for a in args))
    return x
```

```python
import functools
import math

import jax
import jax.numpy as jnp
from jax import lax
from jax.experimental import pallas as pl
from jax.experimental.pallas import tpu as pltpu

D_MODEL = 1024
SSM_WIDTH = 512
SSM_GROUP = 16
SSM_GROUPS = SSM_WIDTH // SSM_GROUP
SSM_STATE = 64
N_HEADS = 8
HEAD_DIM = 64
ATTN_WIDTH = N_HEADS * HEAD_DIM
MOBA_BLOCK = 256
MOBA_TOPK = 3
ROPE_THETA = 500000.0
ROPE_DIM = HEAD_DIM // 4
N_GROUPS = 4
EXPERTS_PER_GROUP = 8
N_EXPERTS = N_GROUPS * EXPERTS_PER_GROUP
D_EXPERT = 128
EPS = 1e-6
IN_WIDTH = SSM_WIDTH + 3 * ATTN_WIDTH + 2 * D_MODEL

LANES = 128
SUBLANES = 8
VMEM_LIMIT = 56 * 1024 * 1024

SSM_CHUNK = 16
SSM_ROW_BLOCK = 128
NEG_BIG = -1e30
MOBA_KV_CHUNK = 1024

_NT = (((1,), (1,)), ((), ()))


def _params(*sem):
    return pltpu.CompilerParams(dimension_semantics=sem, vmem_limit_bytes=VMEM_LIMIT)


def _dot_split(a, b, dims):
    bf, f32 = jnp.bfloat16, jnp.float32
    a_hi, b_hi = a.astype(bf), b.astype(bf)
    a_lo = (a - a_hi.astype(f32)).astype(bf)
    b_lo = (b - b_hi.astype(f32)).astype(bf)
    dot = lambda x, y: lax.dot_general(x, y, dims, preferred_element_type=f32)
    return dot(a_hi, b_hi) + (dot(a_hi, b_lo) + dot(a_lo, b_hi))


def _head_norm_rope(x, g, ones_bd, ct, sa, sb):
    bf, f32 = jnp.bfloat16, jnp.float32
    sq = x * x
    sq_hi = sq.astype(bf)
    sq_lo = (sq - sq_hi.astype(f32)).astype(bf)
    ssq = (jnp.dot(sq_hi, ones_bd, preferred_element_type=f32)
           + jnp.dot(sq_lo, ones_bd, preferred_element_type=f32))
    inv = lax.rsqrt(ssq * (1.0 / HEAD_DIM) + EPS)
    outs = []
    for t in range(x.shape[1] // LANES):
        cols = slice(t * LANES, (t + 1) * LANES)
        y = x[:, cols] * inv[:, cols] * g
        y = y * ct + pltpu.roll(y, LANES - ROPE_DIM // 2, 1) * sa + pltpu.roll(y, ROPE_DIM // 2, 1) * sb
        outs.append(y)
    return jnp.concatenate(outs, axis=1)


def _rope_tables(S):
    half = ROPE_DIM // 2
    f32 = jnp.float32
    d = jnp.arange(LANES) % HEAD_DIM
    rot = d < ROPE_DIM
    inv_freq = jnp.where(rot, ROPE_THETA ** (-(d % half).astype(f32) * 2.0 / ROPE_DIM), 0.0)
    ang = jnp.arange(S, dtype=f32)[:, None] * inv_freq[None, :]
    cos, sin = jnp.cos(ang), jnp.sin(ang)
    ct = jnp.where(rot, cos, 1.0)
    sa = jnp.where(d < half, -sin, 0.0)
    sb = jnp.where(jnp.logical_and(rot, d >= half), sin, 0.0)
    return ct, sa, sb


def _in_proj_kernel(x_ref, g_ref, w_ref, gq_ref, gk_ref, ones_ref, ct_ref, sa_ref, sb_ref,
                    gu_ref, qo_ref, ko_ref, vo_ref, km_ref):
    bf = jnp.bfloat16
    x = x_ref[...]
    ms = jnp.mean(x * x, axis=-1, keepdims=True)
    h = (x * lax.rsqrt(ms + EPS) * g_ref[...]).astype(bf)
    proj = lambda lo, hi: jnp.dot(h, w_ref[:, lo:hi], preferred_element_type=jnp.float32)
    q0 = SSM_WIDTH
    k0, v0, g0 = q0 + ATTN_WIDTH, q0 + 2 * ATTN_WIDTH, q0 + 3 * ATTN_WIDTH
    step = 512
    n_gate = (IN_WIDTH - g0) // step
    for c in range(n_gate):
        gu_ref[:, c * step:(c + 1) * step] = proj(g0 + c * step, g0 + (c + 1) * step)
    gu_ref[:, n_gate * step:] = proj(0, q0)
    ones_bd, ct, sa, sb = ones_ref[...], ct_ref[...], sa_ref[...], sb_ref[...]
    qo_ref[0] = _head_norm_rope(proj(q0, k0), gq_ref[...], ones_bd, ct, sa, sb)
    kn = _head_norm_rope(proj(k0, v0), gk_ref[...], ones_bd, ct, sa, sb)
    ko_ref[0] = kn.astype(bf)
    km_ref[0, 0] = jnp.mean(kn, axis=0, keepdims=True)
    vo_ref[0] = proj(v0, g0).T.astype(bf)


def _in_proj(x2, g, w_bf, gq, gk, B, S):
    ts = MOBA_BLOCK
    nb = S // ts
    T, D = x2.shape
    W = ATTN_WIDTH
    n_gu = IN_WIDTH - 3 * W
    ct, sa, sb = _rope_tables(S)
    head_of = jnp.arange(W) // HEAD_DIM
    ones_bd = (head_of[:, None] == head_of[None, :]).astype(jnp.bfloat16)
    g128 = lambda t: jnp.concatenate([t, t]).reshape(1, LANES)
    const = lambda shape: pl.BlockSpec(shape, lambda b, s: (0,) * len(shape))
    tab = pl.BlockSpec((ts, LANES), lambda b, s: (s, 0))
    tok3 = pl.BlockSpec((1, ts, W), lambda b, s: (b, s, 0))
    return pl.pallas_call(
        _in_proj_kernel,
        grid=(B, nb),
        in_specs=[pl.BlockSpec((ts, D), lambda b, s: (b * nb + s, 0)),
                  const((1, D)), const(w_bf.shape), const((1, LANES)), const((1, LANES)),
                  const((W, W)), tab, tab, tab],
        out_specs=[pl.BlockSpec((ts, n_gu), lambda b, s: (b * nb + s, 0)), tok3, tok3,
                   pl.BlockSpec((1, W, ts), lambda b, s: (b, 0, s)),
                   pl.BlockSpec((1, 1, 1, W), lambda b, s: (b, s, 0, 0))],
        out_shape=[jax.ShapeDtypeStruct((T, n_gu), jnp.float32),
                   jax.ShapeDtypeStruct((B, S, W), jnp.float32),
                   jax.ShapeDtypeStruct((B, S, W), jnp.bfloat16),
                   jax.ShapeDtypeStruct((B, W, S), jnp.bfloat16),
                   jax.ShapeDtypeStruct((B, nb, 1, W), jnp.float32)],
        compiler_params=_params("parallel", "parallel"),
        name="in_proj",
    )(x2, g.reshape(1, D), w_bf, g128(gq), g128(gk), ones_bd, ct, sa, sb)


def _moba_kernel(q_ref, k_ref, vt_ref, km_ref, oh_ref, o_ref, s_ref):
    i = pl.program_id(2)
    tq = q_ref.shape[1]
    nb = km_ref.shape[1]
    bf = jnp.bfloat16
    f32 = jnp.float32
    q = q_ref[0]
    km = km_ref[0]
    lane = lax.broadcasted_iota(jnp.int32, (tq, LANES), 1)
    blk = lax.broadcasted_iota(jnp.int32, (nb, tq), 0)
    past = blk < i
    scale = HEAD_DIM ** -0.5 * math.log2(math.e)

    lane_k = lax.broadcasted_iota(jnp.int32, (nb, LANES), 1)
    km2 = jnp.concatenate([jnp.where(lane_k < HEAD_DIM, km, 0.0),
                           jnp.where(lane_k < HEAD_DIM, 0.0, km)], axis=0)
    scores = _dot_split(km2, q, _NT)

    q_aug = []
    for h in range(2):
        hmask = (lane < HEAD_DIM) if h == 0 else (lane >= HEAD_DIM)
        qh = jnp.where(hmask, q, 0.0)
        s = jnp.where(past, scores[h * nb:(h + 1) * nb], -jnp.inf)
        sel = jnp.zeros((nb, tq), jnp.bool_)
        for _ in range(MOBA_TOPK):
            mx = jnp.max(s, axis=0, keepdims=True)
            first = jnp.min(jnp.where(s == mx, blk, nb), axis=0, keepdims=True)
            pick = blk == first
            sel = jnp.logical_or(sel, jnp.logical_and(pick, past))
            s = jnp.where(pick, -jnp.inf, s)
        bias_t = jnp.where(sel, 0.0, NEG_BIG)
        bias_t = jnp.concatenate([bias_t, jnp.zeros((LANES - nb, tq), f32)], axis=0)
        bias = bias_t.T.astype(bf)
        q_aug.append(jnp.concatenate([(qh * scale).astype(bf), bias], axis=1))

    ck = MOBA_KV_CHUNK
    n_chunks = lax.shift_right_logical(i + (ck // tq - 1), int(math.log2(ck // tq)))
    n_chunks = jnp.maximum(n_chunks, 1)

    def logits(c, slot):
        off = pl.multiple_of(c * ck, ck)
        k_aug = jnp.concatenate([k_ref[0, pl.ds(off, ck), :], oh_ref[pl.ds(off, ck), :]], axis=1)
        cm = []
        for h in range(2):
            s = lax.dot_general(k_aug, q_aug[h], _NT, preferred_element_type=f32)
            s_ref[slot, h] = s
            cm.append(jnp.max(s, axis=0, keepdims=True))
        return cm

    def weigh(s, vt, cm, m, l, acc):
        m_new = cm if m is None else jnp.maximum(m, cm)
        p = jnp.exp2(s - m_new)
        psum = jnp.sum(p, axis=0, keepdims=True)
        pv = jnp.dot(vt, p.astype(bf), preferred_element_type=f32)
        if m is None:
            return [m_new, psum, pv]
        alpha = jnp.exp2(m - m_new)
        return [m_new, alpha * l + psum, alpha * acc + pv]

    def weigh_chunk(c, slot, cm, state):
        off = pl.multiple_of(c * ck, ck)
        out = []
        for h in range(2):
            vt = vt_ref[0, h * HEAD_DIM:(h + 1) * HEAD_DIM, pl.ds(off, ck)]
            out += weigh(s_ref[slot, h], vt, cm[h], *state[3 * h:3 * h + 3])
        return out

    off_d = pl.multiple_of(i * tq, tq)
    kd = k_ref[0, pl.ds(off_d, tq), :]
    key_i = lax.broadcasted_iota(jnp.int32, (tq, tq), 0)
    qry_i = lax.broadcasted_iota(jnp.int32, (tq, tq), 1)
    causal = key_i <= qry_i
    state = []
    for h in range(2):
        s = lax.dot_general(kd, q_aug[h][:, :LANES], _NT, preferred_element_type=f32)
        s = jnp.where(causal, s, -jnp.inf)
        vt = vt_ref[0, h * HEAD_DIM:(h + 1) * HEAD_DIM, pl.ds(off_d, tq)]
        state += weigh(s, vt, jnp.max(s, axis=0, keepdims=True), None, None, None)

    cm0 = logits(0, 0)

    def body(c, carry):
        cm, state = carry[:2], carry[2:]
        state = weigh_chunk(c, c & 1, cm, state)
        cm_next = logits(c + 1, (c + 1) & 1)
        return tuple(cm_next) + tuple(state)

    carry = lax.fori_loop(0, n_chunks - 1, body, tuple(cm0) + tuple(state))
    last = n_chunks - 1
    _, l0, a0, _, l1, a1 = weigh_chunk(last, last & 1, carry[:2], carry[2:])
    o_t = jnp.concatenate([a0 / l0, a1 / l1], axis=0)
    o_ref[0] = o_t.T.astype(o_ref.dtype)


def _moba(q, k_bf, vt_bf, kmean):
    B, S, W = q.shape
    tq = MOBA_BLOCK
    nb = S // tq
    onehot = (jnp.arange(S)[:, None] // tq == jnp.arange(LANES)[None, :]).astype(jnp.bfloat16)
    qo = pl.BlockSpec((1, tq, LANES), lambda b, hp, i: (b, i, hp))
    return pl.pallas_call(
        _moba_kernel,
        grid=(B, W // LANES, nb),
        in_specs=[qo,
                  pl.BlockSpec((1, S, LANES), lambda b, hp, i: (b, 0, hp)),
                  pl.BlockSpec((1, LANES, S), lambda b, hp, i: (b, hp, 0)),
                  pl.BlockSpec((1, nb, LANES), lambda b, hp, i: (b, 0, hp)),
                  pl.BlockSpec((S, LANES), lambda b, hp, i: (0, 0))],
        out_specs=qo,
        out_shape=jax.ShapeDtypeStruct((B, S, W), jnp.bfloat16),
        scratch_shapes=[pltpu.VMEM((2, 2, MOBA_KV_CHUNK, tq), jnp.float32)],
        compiler_params=_params("parallel", "parallel", "arbitrary"),
        name="moba",
    )(q, k_bf, vt_bf, kmean, onehot)


def _ssm_weights(lam_re, lam_im, log_dt, b_re, b_im, c_re, c_im):
    L = SSM_CHUNK
    f32 = jnp.float32
    dt = jnp.exp(log_dt.astype(f32))[:, None]
    lr = jnp.minimum(lam_re.astype(f32), -1e-4)
    li = lam_im.astype(f32)
    mag = jnp.exp(lr * dt)
    ar = mag * jnp.cos(li * dt)
    ai = mag * jnp.sin(li * dt)
    den = lr * lr + li * li
    nr, ni = ar - 1.0, ai
    cr = (nr * lr + ni * li) / den
    ci = (ni * lr - nr * li) / den
    brf, bif = b_re.astype(f32), b_im.astype(f32)
    bbr = cr[..., None] * brf - ci[..., None] * bif
    bbi = cr[..., None] * bif + ci[..., None] * brf
    tau = jnp.arange(L + 1, dtype=f32)[:, None, None]
    pmag = jnp.exp(tau * (lr * dt)[None])
    pr = pmag * jnp.cos(tau * (li * dt)[None])
    pi = pmag * jnp.sin(tau * (li * dt)[None])
    cre, cim = c_re.astype(f32), c_im.astype(f32)
    hi = lax.Precision.HIGHEST
    car = cre[None] * pr[:, :, None, :] - cim[None] * pi[:, :, None, :]
    cai = cre[None] * pi[:, :, None, :] + cim[None] * pr[:, :, None, :]
    kern = (jnp.einsum('tgon,gni->tgoi', car[:L], bbr, precision=hi)
            - jnp.einsum('tgon,gni->tgoi', cai[:L], bbi, precision=hi))
    bf = jnp.bfloat16
    K = L * SSM_GROUP
    lag = jnp.arange(L)[:, None] - jnp.arange(L)[None, :]
    toep = jnp.where((lag >= 0)[:, :, None, None, None],
                     kern.astype(bf)[jnp.clip(lag, 0, L - 1)], 0)
    toep_t = toep.transpose(2, 0, 3, 1, 4).reshape(SSM_GROUPS, K, K)
    rev_r, rev_i = pr[L - 1::-1][:L], pi[L - 1::-1][:L]
    st_r = rev_r[:, :, :, None] * bbr[None] - rev_i[:, :, :, None] * bbi[None]
    st_i = rev_r[:, :, :, None] * bbi[None] + rev_i[:, :, :, None] * bbr[None]
    w_st = jnp.concatenate([st_r, st_i], axis=2)
    w_st_t = w_st.transpose(1, 2, 0, 3).reshape(SSM_GROUPS, 2 * SSM_STATE, K).astype(bf)
    w_out = jnp.concatenate([car[1:], -cai[1:]], axis=3)
    w_out_t = w_out.transpose(1, 0, 2, 3).reshape(SSM_GROUPS, K, 2 * SSM_STATE).astype(bf)
    a_chunk = jnp.concatenate([pr[L], pi[L]], axis=1)
    return toep_t, w_st_t, w_out_t, a_chunk


def _ssm_in_kernel(*refs):
    L, G, C = SSM_CHUNK, SSM_GROUPS, SSM_GROUP
    n_tiles = SSM_WIDTH // LANES
    u_refs, (w_ref, ut_ref, s_ref) = refs[:n_tiles], refs[n_tiles:]
    rb = ut_ref.shape[2]
    gq = LANES // C
    for q, u_ref in enumerate(u_refs):
        for s in range(L):
            xs = u_ref[pl.ds(s, rb, stride=L), :]
            ut_ref[q * gq:(q + 1) * gq, s * C:(s + 1) * C, :] = (
                xs.T.reshape(gq, C, rb).astype(ut_ref.dtype))

    for g in range(G):
        st = jnp.dot(w_ref[g], ut_ref[g], preferred_element_type=jnp.float32)
        s_ref[:, g, :] = st.T


def _ssm_scan_kernel(s_ref, p_ref, q_ref, o_ref):
    n_chunks = s_ref.shape[1]
    pm, qm = p_ref[...], q_ref[...]

    def body(c, carry):
        h, hs = carry
        o_ref[0, c] = h
        s = s_ref[0, c]
        return (h * pm + hs * qm + s, hs * pm - h * qm + pltpu.roll(s, SSM_STATE, 1))

    zero = jnp.zeros(pm.shape, jnp.float32)
    lax.fori_loop(0, n_chunks, body, (zero, zero), unroll=8)


def _ssm_out_kernel(ut_ref, h_ref, t_ref, w_ref, y_ref, yt_ref):
    L, G, C = SSM_CHUNK, SSM_GROUPS, SSM_GROUP
    rb = ut_ref.shape[2]

    for g in range(G):
        yt = jnp.dot(t_ref[g], ut_ref[g], preferred_element_type=jnp.float32)
        yt += lax.dot_general(w_ref[g], h_ref[:, g, :].astype(jnp.bfloat16), _NT,
                              preferred_element_type=jnp.float32)
        yt_ref[g] = yt
    gq = LANES // C
    for q in range(SSM_WIDTH // LANES):
        for t in range(L):
            z = yt_ref[q * gq:(q + 1) * gq, t * C:(t + 1) * C, :].reshape(LANES, rb)
            y_ref[q, pl.ds(t, rb, stride=L), :] = z.T


def _ssm(u_src, toep_t, w_st_t, w_out_t, a_chunk, B, S, u_tile0=0):
    L, G = SSM_CHUNK, SSM_GROUPS
    nc = S // L
    R = B * nc
    K = L * SSM_GROUP
    N2 = 2 * SSM_STATE
    n_tiles = SSM_WIDTH // LANES
    rb = min(SSM_ROW_BLOCK, R)
    bf = jnp.bfloat16
    ut, s_loc = pl.pallas_call(
        _ssm_in_kernel,
        grid=(R // rb,),
        in_specs=[pl.BlockSpec((rb * L, LANES), functools.partial(lambda q, j: (j, q), u_tile0 + q))
                  for q in range(n_tiles)] + [pl.BlockSpec((G, N2, K), lambda j: (0, 0, 0))],
        out_specs=[pl.BlockSpec((G, K, rb), lambda j: (0, 0, j)),
                   pl.BlockSpec((rb, G, N2), lambda j: (j, 0, 0))],
        out_shape=[jax.ShapeDtypeStruct((G, K, R), bf),
                   jax.ShapeDtypeStruct((R, G, N2), jnp.float32)],
        compiler_params=_params("parallel"),
        name="ssm_in",
    )(*([u_src] * n_tiles), w_st_t)
    s_bc = s_loc.reshape(B, nc, G, N2)
    ar, ai = a_chunk[:, :SSM_STATE], a_chunk[:, SSM_STATE:]
    pm = jnp.concatenate([ar, ar], axis=1)
    qm = jnp.concatenate([-ai, ai], axis=1)
    h_in = pl.pallas_call(
        _ssm_scan_kernel,
        grid=(B,),
        in_specs=[pl.BlockSpec((1, nc, G, N2), lambda b: (b, 0, 0, 0)),
                  pl.BlockSpec((G, N2), lambda b: (0, 0)),
                  pl.BlockSpec((G, N2), lambda b: (0, 0))],
        out_specs=pl.BlockSpec((1, nc, G, N2), lambda b: (b, 0, 0, 0)),
        out_shape=jax.ShapeDtypeStruct((B, nc, G, N2), jnp.float32),
        compiler_params=_params("parallel"),
        name="ssm_scan",
    )(s_bc, pm, qm)
    h_g = h_in.reshape(R, G, N2)
    return pl.pallas_call(
        _ssm_out_kernel,
        grid=(R // rb,),
        in_specs=[pl.BlockSpec((G, K, rb), lambda j: (0, 0, j)),
                  pl.BlockSpec((rb, G, N2), lambda j: (j, 0, 0)),
                  pl.BlockSpec((G, K, K), lambda j: (0, 0, 0)),
                  pl.BlockSpec((G, K, N2), lambda j: (0, 0, 0))],
        out_specs=pl.BlockSpec((n_tiles, rb * L, LANES), lambda j: (0, j, 0)),
        out_shape=jax.ShapeDtypeStruct((n_tiles, B * S, LANES), jnp.float32),
        scratch_shapes=[pltpu.VMEM((G, K, rb), jnp.float32)],
        compiler_params=_params("parallel"),
        name="ssm_out",
    )(ut, h_g, toep_t, w_out_t)


def _route(logits):
    tm = logits.shape[0]
    lane = lax.broadcasted_iota(jnp.int32, (tm, LANES), 1)
    is_g = jnp.logical_and(lane >= N_EXPERTS, lane < N_EXPERTS + N_GROUPS)
    gl = jnp.where(is_g, logits, -jnp.inf)
    gmax = jnp.max(gl, axis=-1, keepdims=True)
    ge = jnp.exp(gl - gmax)
    gp = ge / jnp.sum(ge, axis=-1, keepdims=True)
    g_p = jnp.max(gp, axis=-1, keepdims=True)
    g_lane = jnp.min(jnp.where(jnp.logical_and(is_g, gp == g_p), lane, LANES), axis=-1, keepdims=True)
    g_idx = g_lane - N_EXPERTS
    grp_of_lane = lax.shift_right_logical(lane, int(math.log2(EXPERTS_PER_GROUP)))
    in_grp = jnp.logical_and(lane < N_EXPERTS, grp_of_lane == g_idx)
    el = jnp.where(in_grp, logits, -jnp.inf)
    v1 = jnp.max(el, axis=-1, keepdims=True)
    i1 = jnp.min(jnp.where(el == v1, lane, LANES), axis=-1, keepdims=True)
    el2 = jnp.where(lane == i1, -jnp.inf, el)
    v2 = jnp.max(el2, axis=-1, keepdims=True)
    i2 = jnp.min(jnp.where(el2 == v2, lane, LANES), axis=-1, keepdims=True)
    e2 = jnp.exp(v2 - v1)
    den = 1.0 + e2
    w1 = (1.0 / den) * g_p
    w2 = (e2 / den) * g_p
    return jnp.where(lane == i1, w1, jnp.where(lane == i2, w2, 0.0))


def _merge_kernel(y_ref, u_ref, att_ref, gs_ref, ga_ref, x_ref, d_ref, wglu_ref, bglu_ref,
                  wps_ref, wpa_ref, wo_ref, g2_ref, wr_ref, br_ref,
                  x1_ref, h2_ref, comb_ref):
    bf = jnp.bfloat16
    f32 = jnp.float32
    y_tiles = [y_ref[q] for q in range(y_ref.shape[0])]
    y = jnp.concatenate(y_tiles, axis=1) + d_ref[...] * u_ref[...]
    z = jax.nn.gelu(y)
    gate = jnp.dot(z.astype(bf), wglu_ref[...], preferred_element_type=f32) + bglu_ref[...]
    s5 = z * jax.nn.sigmoid(gate)
    y_ssm = jnp.dot(s5.astype(bf), wps_ref[...], preferred_element_type=f32)
    y_att = jnp.dot(att_ref[...], wpa_ref[...], preferred_element_type=f32)
    merged = jax.nn.sigmoid(gs_ref[...]) * y_ssm + jax.nn.sigmoid(ga_ref[...]) * y_att
    x1 = x_ref[...] + jnp.dot(merged.astype(bf), wo_ref[...], preferred_element_type=f32)
    x1_ref[...] = x1
    ms = jnp.mean(x1 * x1, axis=-1, keepdims=True)
    h2 = x1 * lax.rsqrt(ms + EPS) * g2_ref[...]
    h2_ref[...] = h2.astype(bf)
    logits = _dot_split(h2, wr_ref[...], (((1,), (0,)), ((), ()))) + br_ref[...]
    comb_ref[...] = _route(logits)


def _merge(y_raw, gu, att, x2, d, w_glu, b_glu, w_ps, w_pa, w_o, g2, w_r, b_r, tm=512):
    T, D = x2.shape
    W = SSM_WIDTH
    row = lambda shape, col: pl.BlockSpec(shape, lambda i: (i, col))
    full = lambda a: pl.BlockSpec(a.shape, lambda i: (0,) * a.ndim)
    consts = [d.reshape(1, W), w_glu, b_glu.reshape(1, W), w_ps, w_pa, w_o, g2.reshape(1, D), w_r, b_r]
    return pl.pallas_call(
        _merge_kernel,
        grid=(T // tm,),
        in_specs=[pl.BlockSpec((W // LANES, tm, LANES), lambda i: (0, i, 0)), row((tm, W), 2 * D // W), row((tm, W), 0),
                  row((tm, D), 0), row((tm, D), 1), row((tm, D), 0)] + [full(c) for c in consts],
        out_specs=[row((tm, D), 0), row((tm, D), 0), row((tm, LANES), 0)],
        out_shape=[jax.ShapeDtypeStruct((T, D), jnp.float32),
                   jax.ShapeDtypeStruct((T, D), jnp.bfloat16),
                   jax.ShapeDtypeStruct((T, LANES), jnp.float32)],
        compiler_params=_params("parallel"),
        name="merge",
    )(y_raw, gu, att, gu, gu, x2, *consts)


def _moe_kernel(h_ref, comb_ref, x_ref, wg_ref, wu_ref, wd_ref, o_ref, act_ref):
    f32 = jnp.float32
    h = h_ref[...]
    comb = comb_ref[...]
    step = 4 * D_EXPERT
    for c in range(N_EXPERTS * D_EXPERT // step):
        hg = jnp.dot(h, wg_ref[:, c * step:(c + 1) * step], preferred_element_type=f32)
        hu = jnp.dot(h, wu_ref[:, c * step:(c + 1) * step], preferred_element_type=f32)
        a = jax.nn.silu(hg) * hu
        for j in range(step // D_EXPERT):
            e = c * (step // D_EXPERT) + j
            act_ref[:, e * D_EXPERT:(e + 1) * D_EXPERT] = (
                a[:, j * D_EXPERT:(j + 1) * D_EXPERT] * comb[:, e:e + 1]).astype(act_ref.dtype)
    o_ref[...] = x_ref[...] + jnp.dot(act_ref[...], wd_ref[...], preferred_element_type=f32)


def _moe(h2, comb, x1, wg, wu, wd, tm=512):
    T, D = x1.shape
    F = N_EXPERTS * D_EXPERT
    row = lambda w: pl.BlockSpec((tm, w), lambda i: (i, 0))
    once = lambda a: pl.BlockSpec(a.shape, lambda i: (0, 0), pipeline_mode=pl.Buffered(1))
    return pl.pallas_call(
        _moe_kernel,
        grid=(T // tm,),
        in_specs=[row(D), row(LANES), row(D), once(wg), once(wu), once(wd)],
        out_specs=row(D),
        out_shape=jax.ShapeDtypeStruct((T, D), jnp.float32),
        scratch_shapes=[pltpu.VMEM((tm, F), jnp.bfloat16)],
        compiler_params=_params("parallel"),
        name="moe",
    )(h2, comb, x1, wg, wu, wd)


def _layer(x, norm1_g, w_in, lam_re, lam_im, log_dt, ssm_b_re, ssm_b_im, ssm_c_re, ssm_c_im,
           ssm_d, w_glu, b_glu, q_norm_g, k_norm_g, w_proj_ssm, w_proj_attn, w_out, norm2_g,
           w_router_group, b_router_group, w_router_expert, b_router_expert, w_gate, w_up, w_down):
    B, S, D = x.shape
    bf = jnp.bfloat16
    x2 = x.reshape(B * S, D)
    gu, q, k_bf, v_bf, kmean = _in_proj(x2, norm1_g, w_in.astype(bf), q_norm_g, k_norm_g, B, S)
    att = _moba(q, k_bf, v_bf, kmean.reshape(B, S // MOBA_BLOCK, ATTN_WIDTH))
    toep, w_st, w_o_ssm, a_chunk = _ssm_weights(lam_re, lam_im, log_dt, ssm_b_re, ssm_b_im,
                                               ssm_c_re, ssm_c_im)
    y_raw = _ssm(gu, toep, w_st, w_o_ssm, a_chunk, B, S, u_tile0=2 * D // LANES)
    pad = jnp.zeros((D, LANES - N_EXPERTS - N_GROUPS), jnp.float32)
    w_r = jnp.concatenate([w_router_expert, w_router_group, pad], axis=1)
    b_r = jnp.concatenate([b_router_expert, b_router_group,
                           jnp.zeros((LANES - N_EXPERTS - N_GROUPS,), jnp.float32)]).reshape(1, LANES)
    x1, h2, comb = _merge(y_raw, gu, att.reshape(B * S, ATTN_WIDTH), x2, ssm_d,
                          w_glu.astype(bf), b_glu, w_proj_ssm.astype(bf), w_proj_attn.astype(bf),
                          w_out.astype(bf), norm2_g, w_r, b_r)
    F = N_EXPERTS * D_EXPERT
    wg = w_gate.transpose(1, 0, 2).reshape(D, F).astype(bf)
    wu = w_up.transpose(1, 0, 2).reshape(D, F).astype(bf)
    wd = w_down.reshape(F, D).astype(bf)
    out = _moe(h2, comb, x1, wg, wu, wd)
    return out.reshape(B, S, D)


def kernel(x, norm1_g, w_in, lam_re, lam_im, log_dt, ssm_b_re, ssm_b_im, ssm_c_re, ssm_c_im,
           ssm_d, w_glu, b_glu, q_norm_g, k_norm_g, w_proj_ssm, w_proj_attn, w_out, norm2_g,
           w_router_group, b_router_group, w_router_expert, b_router_expert, w_gate, w_up, w_down):
    args = (norm1_g, w_in, lam_re, lam_im, log_dt, ssm_b_re, ssm_b_im, ssm_c_re, ssm_c_im,
            ssm_d, w_glu, b_glu, q_norm_g, k_norm_g, w_proj_ssm, w_proj_attn, w_out, norm2_g,
            w_router_group, b_router_group, w_router_expert, b_router_expert, w_gate, w_up, w_down)
    for l in range(w_in.shape[0]):
        x = _layer(x, *(a[l] for a in args))
    return x
```

```python
import functools
import math

import jax
import jax.numpy as jnp
from jax import lax
from jax.experimental import pallas as pl
from jax.experimental.pallas import tpu as pltpu

D_MODEL = 1024
SSM_WIDTH = 512
SSM_GROUP = 16
SSM_GROUPS = SSM_WIDTH // SSM_GROUP
SSM_STATE = 64
N_HEADS = 8
HEAD_DIM = 64
ATTN_WIDTH = N_HEADS * HEAD_DIM
MOBA_BLOCK = 256
MOBA_TOPK = 3
ROPE_THETA = 500000.0
ROPE_DIM = HEAD_DIM // 4
N_GROUPS = 4
EXPERTS_PER_GROUP = 8
N_EXPERTS = N_GROUPS * EXPERTS_PER_GROUP
D_EXPERT = 128
EPS = 1e-6
IN_WIDTH = SSM_WIDTH + 3 * ATTN_WIDTH + 2 * D_MODEL

LANES = 128
SUBLANES = 8
VMEM_LIMIT = 56 * 1024 * 1024

SSM_CHUNK = 16
SSM_ROW_BLOCK = 128
NEG_BIG = -1e30
MOBA_KV_CHUNK = 1024

_NT = (((1,), (1,)), ((), ()))


def _params(*sem):
    return pltpu.CompilerParams(dimension_semantics=sem, vmem_limit_bytes=VMEM_LIMIT)


def _dot_split(a, b, dims):
    bf, f32 = jnp.bfloat16, jnp.float32
    a_hi, b_hi = a.astype(bf), b.astype(bf)
    a_lo = (a - a_hi.astype(f32)).astype(bf)
    b_lo = (b - b_hi.astype(f32)).astype(bf)
    dot = lambda x, y: lax.dot_general(x, y, dims, preferred_element_type=f32)
    return dot(a_hi, b_hi) + (dot(a_hi, b_lo) + dot(a_lo, b_hi))


def _head_norm_rope(x, g, ones_bd, ct, sa, sb):
    bf, f32 = jnp.bfloat16, jnp.float32
    sq = x * x
    sq_hi = sq.astype(bf)
    sq_lo = (sq - sq_hi.astype(f32)).astype(bf)
    ssq = (jnp.dot(sq_hi, ones_bd, preferred_element_type=f32)
           + jnp.dot(sq_lo, ones_bd, preferred_element_type=f32))
    inv = lax.rsqrt(ssq * (1.0 / HEAD_DIM) + EPS)
    outs = []
    for t in range(x.shape[1] // LANES):
        cols = slice(t * LANES, (t + 1) * LANES)
        y = x[:, cols] * inv[:, cols] * g
        y = y * ct + pltpu.roll(y, LANES - ROPE_DIM // 2, 1) * sa + pltpu.roll(y, ROPE_DIM // 2, 1) * sb
        outs.append(y)
    return jnp.concatenate(outs, axis=1)


def _rope_tables(S):
    half = ROPE_DIM // 2
    f32 = jnp.float32
    d = jnp.arange(LANES) % HEAD_DIM
    rot = d < ROPE_DIM
    inv_freq = jnp.where(rot, ROPE_THETA ** (-(d % half).astype(f32) * 2.0 / ROPE_DIM), 0.0)
    ang = jnp.arange(S, dtype=f32)[:, None] * inv_freq[None, :]
    cos, sin = jnp.cos(ang), jnp.sin(ang)
    ct = jnp.where(rot, cos, 1.0)
    sa = jnp.where(d < half, -sin, 0.0)
    sb = jnp.where(jnp.logical_and(rot, d >= half), sin, 0.0)
    return ct, sa, sb


def _in_proj_kernel(x_ref, g_ref, w_ref, gq_ref, gk_ref, ones_ref, ct_ref, sa_ref, sb_ref,
                    gu_ref, qo_ref, ko_ref, vo_ref, km_ref):
    bf = jnp.bfloat16
    x = x_ref[...]
    ms = jnp.mean(x * x, axis=-1, keepdims=True)
    h = (x * lax.rsqrt(ms + EPS) * g_ref[...]).astype(bf)
    proj = lambda lo, hi: jnp.dot(h, w_ref[:, lo:hi], preferred_element_type=jnp.float32)
    q0 = SSM_WIDTH
    k0, v0, g0 = q0 + ATTN_WIDTH, q0 + 2 * ATTN_WIDTH, q0 + 3 * ATTN_WIDTH
    step = 512
    n_gate = (IN_WIDTH - g0) // step
    for c in range(n_gate):
        gu_ref[:, c * step:(c + 1) * step] = proj(g0 + c * step, g0 + (c + 1) * step)
    gu_ref[:, n_gate * step:] = proj(0, q0)
    ones_bd, ct, sa, sb = ones_ref[...], ct_ref[...], sa_ref[...], sb_ref[...]
    qo_ref[0] = _head_norm_rope(proj(q0, k0), gq_ref[...], ones_bd, ct, sa, sb)
    kn = _head_norm_rope(proj(k0, v0), gk_ref[...], ones_bd, ct, sa, sb)
    ko_ref[0] = kn.astype(bf)
    km_ref[0, 0] = jnp.mean(kn, axis=0, keepdims=True)
    vo_ref[0] = proj(v0, g0).T.astype(bf)


def _in_proj(x2, g, w_bf, gq, gk, B, S):
    ts = MOBA_BLOCK
    nb = S // ts
    T, D = x2.shape
    W = ATTN_WIDTH
    n_gu = IN_WIDTH - 3 * W
    ct, sa, sb = _rope_tables(S)
    head_of = jnp.arange(W) // HEAD_DIM
    ones_bd = (head_of[:, None] == head_of[None, :]).astype(jnp.bfloat16)
    g128 = lambda t: jnp.concatenate([t, t]).reshape(1, LANES)
    const = lambda shape: pl.BlockSpec(shape, lambda b, s: (0,) * len(shape))
    tab = pl.BlockSpec((ts, LANES), lambda b, s: (s, 0))
    tok3 = pl.BlockSpec((1, ts, W), lambda b, s: (b, s, 0))
    return pl.pallas_call(
        _in_proj_kernel,
        grid=(B, nb),
        in_specs=[pl.BlockSpec((ts, D), lambda b, s: (b * nb + s, 0)),
                  const((1, D)), const(w_bf.shape), const((1, LANES)), const((1, LANES)),
                  const((W, W)), tab, tab, tab],
        out_specs=[pl.BlockSpec((ts, n_gu), lambda b, s: (b * nb + s, 0)), tok3, tok3,
                   pl.BlockSpec((1, W, ts), lambda b, s: (b, 0, s)),
                   pl.BlockSpec((1, 1, 1, W), lambda b, s: (b, s, 0, 0))],
        out_shape=[jax.ShapeDtypeStruct((T, n_gu), jnp.float32),
                   jax.ShapeDtypeStruct((B, S, W), jnp.float32),
                   jax.ShapeDtypeStruct((B, S, W), jnp.bfloat16),
                   jax.ShapeDtypeStruct((B, W, S), jnp.bfloat16),
                   jax.ShapeDtypeStruct((B, nb, 1, W), jnp.float32)],
        compiler_params=_params("parallel", "parallel"),
        name="in_proj",
    )(x2, g.reshape(1, D), w_bf, g128(gq), g128(gk), ones_bd, ct, sa, sb)


def _moba_kernel(q_ref, k_ref, vt_ref, km_ref, oh_ref, o_ref, s_ref):
    i = pl.program_id(2)
    tq = q_ref.shape[1]
    nb = km_ref.shape[1]
    bf = jnp.bfloat16
    f32 = jnp.float32
    q = q_ref[0]
    km = km_ref[0]
    lane = lax.broadcasted_iota(jnp.int32, (tq, LANES), 1)
    blk = lax.broadcasted_iota(jnp.int32, (nb, tq), 0)
    past = blk < i
    scale = HEAD_DIM ** -0.5 * math.log2(math.e)

    lane_k = lax.broadcasted_iota(jnp.int32, (nb, LANES), 1)
    km2 = jnp.concatenate([jnp.where(lane_k < HEAD_DIM, km, 0.0),
                           jnp.where(lane_k < HEAD_DIM, 0.0, km)], axis=0)
    scores = _dot_split(km2, q, _NT)

    q_aug = []
    for h in range(2):
        hmask = (lane < HEAD_DIM) if h == 0 else (lane >= HEAD_DIM)
        qh = jnp.where(hmask, q, 0.0)
        s = jnp.where(past, scores[h * nb:(h + 1) * nb], -jnp.inf)
        sel = jnp.zeros((nb, tq), jnp.bool_)
        for _ in range(MOBA_TOPK):
            mx = jnp.max(s, axis=0, keepdims=True)
            first = jnp.min(jnp.where(s == mx, blk, nb), axis=0, keepdims=True)
            pick = blk == first
            sel = jnp.logical_or(sel, jnp.logical_and(pick, past))
            s = jnp.where(pick, -jnp.inf, s)
        bias_t = jnp.where(sel, 0.0, NEG_BIG)
        bias_t = jnp.concatenate([bias_t, jnp.zeros((LANES - nb, tq), f32)], axis=0)
        bias = bias_t.T.astype(bf)
        q_aug.append(jnp.concatenate([(qh * scale).astype(bf), bias], axis=1))

    ck = MOBA_KV_CHUNK
    n_chunks = lax.shift_right_logical(i + (ck // tq - 1), int(math.log2(ck // tq)))
    n_chunks = jnp.maximum(n_chunks, 1)

    def logits(c, slot):
        off = pl.multiple_of(c * ck, ck)
        k_aug = jnp.concatenate([k_ref[0, pl.ds(off, ck), :], oh_ref[pl.ds(off, ck), :]], axis=1)
        cm = []
        for h in range(2):
            s = lax.dot_general(k_aug, q_aug[h], _NT, preferred_element_type=f32)
            s_ref[slot, h] = s
            cm.append(jnp.max(s, axis=0, keepdims=True))
        return cm

    def weigh(s, vt, cm, m, l, acc):
        m_new = cm if m is None else jnp.maximum(m, cm)
        p = jnp.exp2(s - m_new)
        psum = jnp.sum(p, axis=0, keepdims=True)
        pv = jnp.dot(vt, p.astype(bf), preferred_element_type=f32)
        if m is None:
            return [m_new, psum, pv]
        alpha = jnp.exp2(m - m_new)
        return [m_new, alpha * l + psum, alpha * acc + pv]

    def weigh_chunk(c, slot, cm, state):
        off = pl.multiple_of(c * ck, ck)
        out = []
        for h in range(2):
            vt = vt_ref[0, h * HEAD_DIM:(h + 1) * HEAD_DIM, pl.ds(off, ck)]
            out += weigh(s_ref[slot, h], vt, cm[h], *state[3 * h:3 * h + 3])
        return out

    off_d = pl.multiple_of(i * tq, tq)
    kd = k_ref[0, pl.ds(off_d, tq), :]
    key_i = lax.broadcasted_iota(jnp.int32, (tq, tq), 0)
    qry_i = lax.broadcasted_iota(jnp.int32, (tq, tq), 1)
    causal = key_i <= qry_i
    state = []
    for h in range(2):
        s = lax.dot_general(kd, q_aug[h][:, :LANES], _NT, preferred_element_type=f32)
        s = jnp.where(causal, s, -jnp.inf)
        vt = vt_ref[0, h * HEAD_DIM:(h + 1) * HEAD_DIM, pl.ds(off_d, tq)]
        state += weigh(s, vt, jnp.max(s, axis=0, keepdims=True), None, None, None)

    cm0 = logits(0, 0)

    def body(c, carry):
        cm, state = carry[:2], carry[2:]
        state = weigh_chunk(c, c & 1, cm, state)
        cm_next = logits(c + 1, (c + 1) & 1)
        return tuple(cm_next) + tuple(state)

    carry = lax.fori_loop(0, n_chunks - 1, body, tuple(cm0) + tuple(state))
    last = n_chunks - 1
    _, l0, a0, _, l1, a1 = weigh_chunk(last, last & 1, carry[:2], carry[2:])
    o_t = jnp.concatenate([a0 / l0, a1 / l1], axis=0)
    o_ref[0] = o_t.T.astype(o_ref.dtype)


def _moba(q, k_bf, vt_bf, kmean):
    B, S, W = q.shape
    tq = MOBA_BLOCK
    nb = S // tq
    onehot = (jnp.arange(S)[:, None] // tq == jnp.arange(LANES)[None, :]).astype(jnp.bfloat16)
    qo = pl.BlockSpec((1, tq, LANES), lambda b, hp, i: (b, i, hp))
    return pl.pallas_call(
        _moba_kernel,
        grid=(B, W // LANES, nb),
        in_specs=[qo,
                  pl.BlockSpec((1, S, LANES), lambda b, hp, i: (b, 0, hp)),
                  pl.BlockSpec((1, LANES, S), lambda b, hp, i: (b, hp, 0)),
                  pl.BlockSpec((1, nb, LANES), lambda b, hp, i: (b, 0, hp)),
                  pl.BlockSpec((S, LANES), lambda b, hp, i: (0, 0))],
        out_specs=qo,
        out_shape=jax.ShapeDtypeStruct((B, S, W), jnp.bfloat16),
        scratch_shapes=[pltpu.VMEM((2, 2, MOBA_KV_CHUNK, tq), jnp.float32)],
        compiler_params=_params("parallel", "parallel", "arbitrary"),
        name="moba",
    )(q, k_bf, vt_bf, kmean, onehot)


def _ssm_weights(lam_re, lam_im, log_dt, b_re, b_im, c_re, c_im):
    L = SSM_CHUNK
    f32 = jnp.float32
    dt = jnp.exp(log_dt.astype(f32))[:, None]
    lr = jnp.minimum(lam_re.astype(f32), -1e-4)
    li = lam_im.astype(f32)
    mag = jnp.exp(lr * dt)
    ar = mag * jnp.cos(li * dt)
    ai = mag * jnp.sin(li * dt)
    den = lr * lr + li * li
    nr, ni = ar - 1.0, ai
    cr = (nr * lr + ni * li) / den
    ci = (ni * lr - nr * li) / den
    brf, bif = b_re.astype(f32), b_im.astype(f32)
    bbr = cr[..., None] * brf - ci[..., None] * bif
    bbi = cr[..., None] * bif + ci[..., None] * brf
    tau = jnp.arange(L + 1, dtype=f32)[:, None, None]
    pmag = jnp.exp(tau * (lr * dt)[None])
    pr = pmag * jnp.cos(tau * (li * dt)[None])
    pi = pmag * jnp.sin(tau * (li * dt)[None])
    cre, cim = c_re.astype(f32), c_im.astype(f32)
    hi = lax.Precision.HIGHEST
    car = cre[None] * pr[:, :, None, :] - cim[None] * pi[:, :, None, :]
    cai = cre[None] * pi[:, :, None, :] + cim[None] * pr[:, :, None, :]
    kern = (jnp.einsum('tgon,gni->tgoi', car[:L], bbr, precision=hi)
            - jnp.einsum('tgon,gni->tgoi', cai[:L], bbi, precision=hi))
    bf = jnp.bfloat16
    K = L * SSM_GROUP
    C = SSM_GROUP
    P = 2 * K
    flat = jnp.concatenate([kern[::-1], jnp.zeros_like(kern)], axis=0).astype(bf)
    flat = flat.transpose(1, 2, 0, 3).reshape(SSM_GROUPS, C, P)
    win = jnp.tile(flat, (1, 1, L + 1))[..., :L * (P + C)].reshape(SSM_GROUPS, C, L, P + C)[..., :K]
    toep_t = win[:, :, ::-1].transpose(0, 2, 1, 3).reshape(SSM_GROUPS, K, K)
    rev_r, rev_i = pr[L - 1::-1][:L], pi[L - 1::-1][:L]
    st_r = rev_r[:, :, :, None] * bbr[None] - rev_i[:, :, :, None] * bbi[None]
    st_i = rev_r[:, :, :, None] * bbi[None] + rev_i[:, :, :, None] * bbr[None]
    w_st = jnp.concatenate([st_r, st_i], axis=2)
    w_st_t = w_st.transpose(1, 2, 0, 3).reshape(SSM_GROUPS, 2 * SSM_STATE, K).astype(bf)
    w_out = jnp.concatenate([car[1:], -cai[1:]], axis=3)
    w_out_t = w_out.transpose(1, 0, 2, 3).reshape(SSM_GROUPS, K, 2 * SSM_STATE).astype(bf)
    a_chunk = jnp.concatenate([pr[L], pi[L]], axis=1)
    return toep_t, w_st_t, w_out_t, a_chunk


def _ssm_in_kernel(*refs):
    L, G, C = SSM_CHUNK, SSM_GROUPS, SSM_GROUP
    n_tiles = SSM_WIDTH // LANES
    u_refs, (w_ref, ut_ref, s_ref) = refs[:n_tiles], refs[n_tiles:]
    rb = ut_ref.shape[2]
    gq = LANES // C
    for q, u_ref in enumerate(u_refs):
        for s in range(L):
            xs = u_ref[pl.ds(s, rb, stride=L), :]
            ut_ref[q * gq:(q + 1) * gq, s * C:(s + 1) * C, :] = (
                xs.T.reshape(gq, C, rb).astype(ut_ref.dtype))

    for g in range(G):
        st = jnp.dot(w_ref[g], ut_ref[g], preferred_element_type=jnp.float32)
        s_ref[:, g, :] = st.T


def _ssm_scan_kernel(s_ref, p_ref, q_ref, o_ref):
    n_chunks = s_ref.shape[1]
    pm, qm = p_ref[...], q_ref[...]

    def body(c, carry):
        h, hs = carry
        o_ref[0, c] = h
        s = s_ref[0, c]
        return (h * pm + hs * qm + s, hs * pm - h * qm + pltpu.roll(s, SSM_STATE, 1))

    zero = jnp.zeros(pm.shape, jnp.float32)
    lax.fori_loop(0, n_chunks, body, (zero, zero), unroll=8)


def _ssm_out_kernel(ut_ref, h_ref, t_ref, w_ref, y_ref, yt_ref):
    L, G, C = SSM_CHUNK, SSM_GROUPS, SSM_GROUP
    rb = ut_ref.shape[2]

    for g in range(G):
        yt = jnp.dot(t_ref[g], ut_ref[g], preferred_element_type=jnp.float32)
        yt += lax.dot_general(w_ref[g], h_ref[:, g, :].astype(jnp.bfloat16), _NT,
                              preferred_element_type=jnp.float32)
        yt_ref[g] = yt
    gq = LANES // C
    for q in range(SSM_WIDTH // LANES):
        for t in range(L):
            z = yt_ref[q * gq:(q + 1) * gq, t * C:(t + 1) * C, :].reshape(LANES, rb)
            y_ref[q, pl.ds(t, rb, stride=L), :] = z.T


def _ssm(u_src, toep_t, w_st_t, w_out_t, a_chunk, B, S, u_tile0=0):
    L, G = SSM_CHUNK, SSM_GROUPS
    nc = S // L
    R = B * nc
    K = L * SSM_GROUP
    N2 = 2 * SSM_STATE
    n_tiles = SSM_WIDTH // LANES
    rb = min(SSM_ROW_BLOCK, R)
    bf = jnp.bfloat16
    ut, s_loc = pl.pallas_call(
        _ssm_in_kernel,
        grid=(R // rb,),
        in_specs=[pl.BlockSpec((rb * L, LANES), functools.partial(lambda q, j: (j, q), u_tile0 + q))
                  for q in range(n_tiles)] + [pl.BlockSpec((G, N2, K), lambda j: (0, 0, 0))],
        out_specs=[pl.BlockSpec((G, K, rb), lambda j: (0, 0, j)),
                   pl.BlockSpec((rb, G, N2), lambda j: (j, 0, 0))],
        out_shape=[jax.ShapeDtypeStruct((G, K, R), bf),
                   jax.ShapeDtypeStruct((R, G, N2), jnp.float32)],
        compiler_params=_params("parallel"),
        name="ssm_in",
    )(*([u_src] * n_tiles), w_st_t)
    s_bc = s_loc.reshape(B, nc, G, N2)
    ar, ai = a_chunk[:, :SSM_STATE], a_chunk[:, SSM_STATE:]
    pm = jnp.concatenate([ar, ar], axis=1)
    qm = jnp.concatenate([-ai, ai], axis=1)
    h_in = pl.pallas_call(
        _ssm_scan_kernel,
        grid=(B,),
        in_specs=[pl.BlockSpec((1, nc, G, N2), lambda b: (b, 0, 0, 0)),
                  pl.BlockSpec((G, N2), lambda b: (0, 0)),
                  pl.BlockSpec((G, N2), lambda b: (0, 0))],
        out_specs=pl.BlockSpec((1, nc, G, N2), lambda b: (b, 0, 0, 0)),
        out_shape=jax.ShapeDtypeStruct((B, nc, G, N2), jnp.float32),
        compiler_params=_params("parallel"),
        name="ssm_scan",
    )(s_bc, pm, qm)
    h_g = h_in.reshape(R, G, N2)
    return pl.pallas_call(
        _ssm_out_kernel,
        grid=(R // rb,),
        in_specs=[pl.BlockSpec((G, K, rb), lambda j: (0, 0, j)),
                  pl.BlockSpec((rb, G, N2), lambda j: (j, 0, 0)),
                  pl.BlockSpec((G, K, K), lambda j: (0, 0, 0)),
                  pl.BlockSpec((G, K, N2), lambda j: (0, 0, 0))],
        out_specs=pl.BlockSpec((n_tiles, rb * L, LANES), lambda j: (0, j, 0)),
        out_shape=jax.ShapeDtypeStruct((n_tiles, B * S, LANES), jnp.float32),
        scratch_shapes=[pltpu.VMEM((G, K, rb), jnp.float32)],
        compiler_params=_params("parallel"),
        name="ssm_out",
    )(ut, h_g, toep_t, w_out_t)


def _route(logits):
    tm = logits.shape[0]
    lane = lax.broadcasted_iota(jnp.int32, (tm, LANES), 1)
    is_g = jnp.logical_and(lane >= N_EXPERTS, lane < N_EXPERTS + N_GROUPS)
    gl = jnp.where(is_g, logits, -jnp.inf)
    gmax = jnp.max(gl, axis=-1, keepdims=True)
    ge = jnp.exp(gl - gmax)
    gp = ge / jnp.sum(ge, axis=-1, keepdims=True)
    g_p = jnp.max(gp, axis=-1, keepdims=True)
    g_lane = jnp.min(jnp.where(jnp.logical_and(is_g, gp == g_p), lane, LANES), axis=-1, keepdims=True)
    g_idx = g_lane - N_EXPERTS
    grp_of_lane = lax.shift_right_logical(lane, int(math.log2(EXPERTS_PER_GROUP)))
    in_grp = jnp.logical_and(lane < N_EXPERTS, grp_of_lane == g_idx)
    el = jnp.where(in_grp, logits, -jnp.inf)
    v1 = jnp.max(el, axis=-1, keepdims=True)
    i1 = jnp.min(jnp.where(el == v1, lane, LANES), axis=-1, keepdims=True)
    el2 = jnp.where(lane == i1, -jnp.inf, el)
    v2 = jnp.max(el2, axis=-1, keepdims=True)
    i2 = jnp.min(jnp.where(el2 == v2, lane, LANES), axis=-1, keepdims=True)
    e2 = jnp.exp(v2 - v1)
    den = 1.0 + e2
    w1 = (1.0 / den) * g_p
    w2 = (e2 / den) * g_p
    return jnp.where(lane == i1, w1, jnp.where(lane == i2, w2, 0.0))


def _merge_kernel(y_ref, u_ref, att_ref, gs_ref, ga_ref, x_ref, d_ref, wglu_ref, bglu_ref,
                  wps_ref, wpa_ref, wo_ref, g2_ref, wr_ref, br_ref,
                  x1_ref, h2_ref, comb_ref):
    bf = jnp.bfloat16
    f32 = jnp.float32
    y_tiles = [y_ref[q] for q in range(y_ref.shape[0])]
    y = jnp.concatenate(y_tiles, axis=1) + d_ref[...] * u_ref[...]
    z = jax.nn.gelu(y)
    gate = jnp.dot(z.astype(bf), wglu_ref[...], preferred_element_type=f32) + bglu_ref[...]
    s5 = z * jax.nn.sigmoid(gate)
    y_ssm = jnp.dot(s5.astype(bf), wps_ref[...], preferred_element_type=f32)
    y_att = jnp.dot(att_ref[...], wpa_ref[...], preferred_element_type=f32)
    merged = jax.nn.sigmoid(gs_ref[...]) * y_ssm + jax.nn.sigmoid(ga_ref[...]) * y_att
    x1 = x_ref[...] + jnp.dot(merged.astype(bf), wo_ref[...], preferred_element_type=f32)
    x1_ref[...] = x1
    ms = jnp.mean(x1 * x1, axis=-1, keepdims=True)
    h2 = x1 * lax.rsqrt(ms + EPS) * g2_ref[...]
    h2_ref[...] = h2.astype(bf)
    logits = _dot_split(h2, wr_ref[...], (((1,), (0,)), ((), ()))) + br_ref[...]
    comb_ref[...] = _route(logits)


def _merge(y_raw, gu, att, x2, d, w_glu, b_glu, w_ps, w_pa, w_o, g2, w_r, b_r, tm=512):
    T, D = x2.shape
    W = SSM_WIDTH
    row = lambda shape, col: pl.BlockSpec(shape, lambda i: (i, col))
    full = lambda a: pl.BlockSpec(a.shape, lambda i: (0,) * a.ndim)
    consts = [d.reshape(1, W), w_glu, b_glu.reshape(1, W), w_ps, w_pa, w_o, g2.reshape(1, D), w_r, b_r]
    return pl.pallas_call(
        _merge_kernel,
        grid=(T // tm,),
        in_specs=[pl.BlockSpec((W // LANES, tm, LANES), lambda i: (0, i, 0)), row((tm, W), 2 * D // W), row((tm, W), 0),
                  row((tm, D), 0), row((tm, D), 1), row((tm, D), 0)] + [full(c) for c in consts],
        out_specs=[row((tm, D), 0), row((tm, D), 0), row((tm, LANES), 0)],
        out_shape=[jax.ShapeDtypeStruct((T, D), jnp.float32),
                   jax.ShapeDtypeStruct((T, D), jnp.bfloat16),
                   jax.ShapeDtypeStruct((T, LANES), jnp.float32)],
        compiler_params=_params("parallel"),
        name="merge",
    )(y_raw, gu, att, gu, gu, x2, *consts)


def _moe_kernel(h_ref, comb_ref, x_ref, wg_ref, wu_ref, wd_ref, o_ref, act_ref):
    f32 = jnp.float32
    h = h_ref[...]
    comb = comb_ref[...]
    step = 4 * D_EXPERT
    for c in range(N_EXPERTS * D_EXPERT // step):
        hg = jnp.dot(h, wg_ref[:, c * step:(c + 1) * step], preferred_element_type=f32)
        hu = jnp.dot(h, wu_ref[:, c * step:(c + 1) * step], preferred_element_type=f32)
        a = jax.nn.silu(hg) * hu
        for j in range(step // D_EXPERT):
            e = c * (step // D_EXPERT) + j
            act_ref[:, e * D_EXPERT:(e + 1) * D_EXPERT] = (
                a[:, j * D_EXPERT:(j + 1) * D_EXPERT] * comb[:, e:e + 1]).astype(act_ref.dtype)
    o_ref[...] = x_ref[...] + jnp.dot(act_ref[...], wd_ref[...], preferred_element_type=f32)


def _moe(h2, comb, x1, wg, wu, wd, tm=512):
    T, D = x1.shape
    F = N_EXPERTS * D_EXPERT
    row = lambda w: pl.BlockSpec((tm, w), lambda i: (i, 0))
    once = lambda a: pl.BlockSpec(a.shape, lambda i: (0, 0), pipeline_mode=pl.Buffered(1))
    return pl.pallas_call(
        _moe_kernel,
        grid=(T // tm,),
        in_specs=[row(D), row(LANES), row(D), once(wg), once(wu), once(wd)],
        out_specs=row(D),
        out_shape=jax.ShapeDtypeStruct((T, D), jnp.float32),
        scratch_shapes=[pltpu.VMEM((tm, F), jnp.bfloat16)],
        compiler_params=_params("parallel"),
        name="moe",
    )(h2, comb, x1, wg, wu, wd)


def _layer(x, norm1_g, w_in, lam_re, lam_im, log_dt, ssm_b_re, ssm_b_im, ssm_c_re, ssm_c_im,
           ssm_d, w_glu, b_glu, q_norm_g, k_norm_g, w_proj_ssm, w_proj_attn, w_out, norm2_g,
           w_router_group, b_router_group, w_router_expert, b_router_expert, w_gate, w_up, w_down):
    B, S, D = x.shape
    bf = jnp.bfloat16
    x2 = x.reshape(B * S, D)
    gu, q, k_bf, v_bf, kmean = _in_proj(x2, norm1_g, w_in.astype(bf), q_norm_g, k_norm_g, B, S)
    att = _moba(q, k_bf, v_bf, kmean.reshape(B, S // MOBA_BLOCK, ATTN_WIDTH))
    toep, w_st, w_o_ssm, a_chunk = _ssm_weights(lam_re, lam_im, log_dt, ssm_b_re, ssm_b_im,
                                               ssm_c_re, ssm_c_im)
    y_raw = _ssm(gu, toep, w_st, w_o_ssm, a_chunk, B, S, u_tile0=2 * D // LANES)
    pad = jnp.zeros((D, LANES - N_EXPERTS - N_GROUPS), jnp.float32)
    w_r = jnp.concatenate([w_router_expert, w_router_group, pad], axis=1)
    b_r = jnp.concatenate([b_router_expert, b_router_group,
                           jnp.zeros((LANES - N_EXPERTS - N_GROUPS,), jnp.float32)]).reshape(1, LANES)
    x1, h2, comb = _merge(y_raw, gu, att.reshape(B * S, ATTN_WIDTH), x2, ssm_d,
                          w_glu.astype(bf), b_glu, w_proj_ssm.astype(bf), w_proj_attn.astype(bf),
                          w_out.astype(bf), norm2_g, w_r, b_r)
    F = N_EXPERTS * D_EXPERT
    wg = w_gate.transpose(1, 0, 2).reshape(D, F).astype(bf)
    wu = w_up.transpose(1, 0, 2).reshape(D, F).astype(bf)
    wd = w_down.reshape(F, D).astype(bf)
    out = _moe(h2, comb, x1, wg, wu, wd)
    return out.reshape(B, S, D)


def kernel(x, norm1_g, w_in, lam_re, lam_im, log_dt, ssm_b_re, ssm_b_im, ssm_c_re, ssm_c_im,
           ssm_d, w_glu, b_glu, q_norm_g, k_norm_g, w_proj_ssm, w_proj_attn, w_out, norm2_g,
           w_router_group, b_router_group, w_router_expert, b_router_expert, w_gate, w_up, w_down):
    args = (norm1_g, w_in, lam_re, lam_im, log_dt, ssm_b_re, ssm_b_im, ssm_c_re, ssm_c_im,
            ssm_d, w_glu, b_glu, q_norm_g, k_norm_g, w_proj_ssm, w_proj_attn, w_out, norm2_g,
            w_router_group, b_router_group, w_router_expert, b_router_expert, w_gate, w_up, w_down)
    for l in range(w_in.shape[0]):
        x = _layer(x, *(a[l] for a in args))
    return x
```

```python
import functools
import math

import jax
import jax.numpy as jnp
from jax import lax
from jax.experimental import pallas as pl
from jax.experimental.pallas import tpu as pltpu

D_MODEL = 1024
SSM_WIDTH = 512
SSM_GROUP = 16
SSM_GROUPS = SSM_WIDTH // SSM_GROUP
SSM_STATE = 64
N_HEADS = 8
HEAD_DIM = 64
ATTN_WIDTH = N_HEADS * HEAD_DIM
MOBA_BLOCK = 256
MOBA_TOPK = 3
ROPE_THETA = 500000.0
ROPE_DIM = HEAD_DIM // 4
N_GROUPS = 4
EXPERTS_PER_GROUP = 8
N_EXPERTS = N_GROUPS * EXPERTS_PER_GROUP
D_EXPERT = 128
EPS = 1e-6
IN_WIDTH = SSM_WIDTH + 3 * ATTN_WIDTH + 2 * D_MODEL

LANES = 128
SUBLANES = 8
VMEM_LIMIT = 56 * 1024 * 1024

SSM_CHUNK = 16
SSM_ROW_BLOCK = 128
NEG_BIG = -1e30
MOBA_KV_CHUNK = 1024

_NT = (((1,), (1,)), ((), ()))


def _params(*sem):
    return pltpu.CompilerParams(dimension_semantics=sem, vmem_limit_bytes=VMEM_LIMIT)


def _dot_split(a, b, dims):
    bf, f32 = jnp.bfloat16, jnp.float32
    a_hi, b_hi = a.astype(bf), b.astype(bf)
    a_lo = (a - a_hi.astype(f32)).astype(bf)
    b_lo = (b - b_hi.astype(f32)).astype(bf)
    dot = lambda x, y: lax.dot_general(x, y, dims, preferred_element_type=f32)
    return dot(a_hi, b_hi) + (dot(a_hi, b_lo) + dot(a_lo, b_hi))


def _head_norm_rope(x, g, ones_bd, ct, sa, sb):
    bf, f32 = jnp.bfloat16, jnp.float32
    sq = x * x
    sq_hi = sq.astype(bf)
    sq_lo = (sq - sq_hi.astype(f32)).astype(bf)
    ssq = (jnp.dot(sq_hi, ones_bd, preferred_element_type=f32)
           + jnp.dot(sq_lo, ones_bd, preferred_element_type=f32))
    inv = lax.rsqrt(ssq * (1.0 / HEAD_DIM) + EPS)
    outs = []
    for t in range(x.shape[1] // LANES):
        cols = slice(t * LANES, (t + 1) * LANES)
        y = x[:, cols] * inv[:, cols] * g
        y = y * ct + pltpu.roll(y, LANES - ROPE_DIM // 2, 1) * sa + pltpu.roll(y, ROPE_DIM // 2, 1) * sb
        outs.append(y)
    return jnp.concatenate(outs, axis=1)


def _rope_tables(S):
    half = ROPE_DIM // 2
    f32 = jnp.float32
    d = jnp.arange(LANES) % HEAD_DIM
    rot = d < ROPE_DIM
    inv_freq = jnp.where(rot, ROPE_THETA ** (-(d % half).astype(f32) * 2.0 / ROPE_DIM), 0.0)
    ang = jnp.arange(S, dtype=f32)[:, None] * inv_freq[None, :]
    cos, sin = jnp.cos(ang), jnp.sin(ang)
    ct = jnp.where(rot, cos, 1.0)
    sa = jnp.where(d < half, -sin, 0.0)
    sb = jnp.where(jnp.logical_and(rot, d >= half), sin, 0.0)
    return ct, sa, sb


def _in_proj_kernel(x_ref, g_ref, w_ref, gq_ref, gk_ref, ones_ref, ct_ref, sa_ref, sb_ref,
                    gu_ref, qo_ref, ko_ref, vo_ref, km_ref):
    bf = jnp.bfloat16
    x = x_ref[...]
    ms = jnp.mean(x * x, axis=-1, keepdims=True)
    h = (x * lax.rsqrt(ms + EPS) * g_ref[...]).astype(bf)
    proj = lambda lo, hi: jnp.dot(h, w_ref[:, lo:hi], preferred_element_type=jnp.float32)
    q0 = SSM_WIDTH
    k0, v0, g0 = q0 + ATTN_WIDTH, q0 + 2 * ATTN_WIDTH, q0 + 3 * ATTN_WIDTH
    step = 512
    n_gate = (IN_WIDTH - g0) // step
    for c in range(n_gate):
        gu_ref[:, c * step:(c + 1) * step] = proj(g0 + c * step, g0 + (c + 1) * step)
    gu_ref[:, n_gate * step:] = proj(0, q0)
    ones_bd, ct, sa, sb = ones_ref[...], ct_ref[...], sa_ref[...], sb_ref[...]
    qo_ref[0] = _head_norm_rope(proj(q0, k0), gq_ref[...], ones_bd, ct, sa, sb)
    kn = _head_norm_rope(proj(k0, v0), gk_ref[...], ones_bd, ct, sa, sb)
    ko_ref[0] = kn.astype(bf)
    km_ref[0, 0] = jnp.mean(kn, axis=0, keepdims=True)
    vo_ref[0] = proj(v0, g0).T.astype(bf)


def _in_proj(x2, g, w_bf, gq, gk, B, S):
    ts = MOBA_BLOCK
    nb = S // ts
    T, D = x2.shape
    W = ATTN_WIDTH
    n_gu = IN_WIDTH - 3 * W
    ct, sa, sb = _rope_tables(S)
    head_of = jnp.arange(W) // HEAD_DIM
    ones_bd = (head_of[:, None] == head_of[None, :]).astype(jnp.bfloat16)
    g128 = lambda t: jnp.concatenate([t, t]).reshape(1, LANES)
    const = lambda shape: pl.BlockSpec(shape, lambda b, s: (0,) * len(shape))
    tab = pl.BlockSpec((ts, LANES), lambda b, s: (s, 0))
    tok3 = pl.BlockSpec((1, ts, W), lambda b, s: (b, s, 0))
    return pl.pallas_call(
        _in_proj_kernel,
        grid=(B, nb),
        in_specs=[pl.BlockSpec((ts, D), lambda b, s: (b * nb + s, 0)),
                  const((1, D)), const(w_bf.shape), const((1, LANES)), const((1, LANES)),
                  const((W, W)), tab, tab, tab],
        out_specs=[pl.BlockSpec((ts, n_gu), lambda b, s: (b * nb + s, 0)), tok3, tok3,
                   pl.BlockSpec((1, W, ts), lambda b, s: (b, 0, s)),
                   pl.BlockSpec((1, 1, 1, W), lambda b, s: (b, s, 0, 0))],
        out_shape=[jax.ShapeDtypeStruct((T, n_gu), jnp.float32),
                   jax.ShapeDtypeStruct((B, S, W), jnp.float32),
                   jax.ShapeDtypeStruct((B, S, W), jnp.bfloat16),
                   jax.ShapeDtypeStruct((B, W, S), jnp.bfloat16),
                   jax.ShapeDtypeStruct((B, nb, 1, W), jnp.float32)],
        compiler_params=_params("parallel", "parallel"),
        name="in_proj",
    )(x2, g.reshape(1, D), w_bf, g128(gq), g128(gk), ones_bd, ct, sa, sb)


def _moba_kernel(q_ref, k_ref, vt_ref, km_ref, oh_ref, o_ref, s_ref):
    i = pl.program_id(2)
    tq = q_ref.shape[1]
    nb = km_ref.shape[1]
    bf = jnp.bfloat16
    f32 = jnp.float32
    q = q_ref[0]
    km = km_ref[0]
    lane = lax.broadcasted_iota(jnp.int32, (tq, LANES), 1)
    blk = lax.broadcasted_iota(jnp.int32, (nb, tq), 0)
    past = blk < i
    scale = HEAD_DIM ** -0.5 * math.log2(math.e)

    lane_k = lax.broadcasted_iota(jnp.int32, (nb, LANES), 1)
    km2 = jnp.concatenate([jnp.where(lane_k < HEAD_DIM, km, 0.0),
                           jnp.where(lane_k < HEAD_DIM, 0.0, km)], axis=0)
    scores = _dot_split(km2, q, _NT)

    q_aug = []
    for h in range(2):
        hmask = (lane < HEAD_DIM) if h == 0 else (lane >= HEAD_DIM)
        qh = jnp.where(hmask, q, 0.0)
        s = jnp.where(past, scores[h * nb:(h + 1) * nb], -jnp.inf)
        sel = jnp.zeros((nb, tq), jnp.bool_)
        for _ in range(MOBA_TOPK):
            mx = jnp.max(s, axis=0, keepdims=True)
            first = jnp.min(jnp.where(s == mx, blk, nb), axis=0, keepdims=True)
            pick = blk == first
            sel = jnp.logical_or(sel, jnp.logical_and(pick, past))
            s = jnp.where(pick, -jnp.inf, s)
        bias_t = jnp.where(sel, 0.0, NEG_BIG)
        bias_t = jnp.concatenate([bias_t, jnp.zeros((LANES - nb, tq), f32)], axis=0)
        bias = bias_t.T.astype(bf)
        q_aug.append(jnp.concatenate([(qh * scale).astype(bf), bias], axis=1))

    ck = MOBA_KV_CHUNK
    n_chunks = lax.shift_right_logical(i + (ck // tq - 1), int(math.log2(ck // tq)))
    n_chunks = jnp.maximum(n_chunks, 1)

    def logits(c, slot):
        off = pl.multiple_of(c * ck, ck)
        k_aug = jnp.concatenate([k_ref[0, pl.ds(off, ck), :], oh_ref[pl.ds(off, ck), :]], axis=1)
        cm = []
        for h in range(2):
            s = lax.dot_general(k_aug, q_aug[h], _NT, preferred_element_type=f32)
            s_ref[slot, h] = s
            cm.append(jnp.max(s, axis=0, keepdims=True))
        return cm

    def weigh(s, vt, cm, m, l, acc):
        m_new = cm if m is None else jnp.maximum(m, cm)
        p = jnp.exp2(s - m_new)
        psum = jnp.sum(p, axis=0, keepdims=True)
        pv = jnp.dot(vt, p.astype(bf), preferred_element_type=f32)
        if m is None:
            return [m_new, psum, pv]
        alpha = jnp.exp2(m - m_new)
        return [m_new, alpha * l + psum, alpha * acc + pv]

    def weigh_chunk(c, slot, cm, state):
        off = pl.multiple_of(c * ck, ck)
        out = []
        for h in range(2):
            vt = vt_ref[0, h * HEAD_DIM:(h + 1) * HEAD_DIM, pl.ds(off, ck)]
            out += weigh(s_ref[slot, h], vt, cm[h], *state[3 * h:3 * h + 3])
        return out

    off_d = pl.multiple_of(i * tq, tq)
    kd = k_ref[0, pl.ds(off_d, tq), :]
    key_i = lax.broadcasted_iota(jnp.int32, (tq, tq), 0)
    qry_i = lax.broadcasted_iota(jnp.int32, (tq, tq), 1)
    causal = key_i <= qry_i
    state = []
    for h in range(2):
        s = lax.dot_general(kd, q_aug[h][:, :LANES], _NT, preferred_element_type=f32)
        s = jnp.where(causal, s, -jnp.inf)
        vt = vt_ref[0, h * HEAD_DIM:(h + 1) * HEAD_DIM, pl.ds(off_d, tq)]
        state += weigh(s, vt, jnp.max(s, axis=0, keepdims=True), None, None, None)

    cm0 = logits(0, 0)

    def body(c, carry):
        cm, state = carry[:2], carry[2:]
        state = weigh_chunk(c, c & 1, cm, state)
        cm_next = logits(c + 1, (c + 1) & 1)
        return tuple(cm_next) + tuple(state)

    carry = lax.fori_loop(0, n_chunks - 1, body, tuple(cm0) + tuple(state))
    last = n_chunks - 1
    _, l0, a0, _, l1, a1 = weigh_chunk(last, last & 1, carry[:2], carry[2:])
    o_t = jnp.concatenate([a0 / l0, a1 / l1], axis=0)
    o_ref[0] = o_t.T.astype(o_ref.dtype)


def _moba(q, k_bf, vt_bf, kmean):
    B, S, W = q.shape
    tq = MOBA_BLOCK
    nb = S // tq
    onehot = (jnp.arange(S)[:, None] // tq == jnp.arange(LANES)[None, :]).astype(jnp.bfloat16)
    qo = pl.BlockSpec((1, tq, LANES), lambda b, hp, i: (b, i, hp))
    return pl.pallas_call(
        _moba_kernel,
        grid=(B, W // LANES, nb),
        in_specs=[qo,
                  pl.BlockSpec((1, S, LANES), lambda b, hp, i: (b, 0, hp)),
                  pl.BlockSpec((1, LANES, S), lambda b, hp, i: (b, hp, 0)),
                  pl.BlockSpec((1, nb, LANES), lambda b, hp, i: (b, 0, hp)),
                  pl.BlockSpec((S, LANES), lambda b, hp, i: (0, 0))],
        out_specs=qo,
        out_shape=jax.ShapeDtypeStruct((B, S, W), jnp.bfloat16),
        scratch_shapes=[pltpu.VMEM((2, 2, MOBA_KV_CHUNK, tq), jnp.float32)],
        compiler_params=_params("parallel", "parallel", "arbitrary"),
        name="moba",
    )(q, k_bf, vt_bf, kmean, onehot)


def _ssm_weights(lam_re, lam_im, log_dt, b_re, b_im, c_re, c_im):
    L = SSM_CHUNK
    f32 = jnp.float32
    dt = jnp.exp(log_dt.astype(f32))[:, None]
    lr = jnp.minimum(lam_re.astype(f32), -1e-4)
    li = lam_im.astype(f32)
    mag = jnp.exp(lr * dt)
    ar = mag * jnp.cos(li * dt)
    ai = mag * jnp.sin(li * dt)
    den = lr * lr + li * li
    nr, ni = ar - 1.0, ai
    cr = (nr * lr + ni * li) / den
    ci = (ni * lr - nr * li) / den
    brf, bif = b_re.astype(f32), b_im.astype(f32)
    bbr = cr[..., None] * brf - ci[..., None] * bif
    bbi = cr[..., None] * bif + ci[..., None] * brf
    tau = jnp.arange(L + 1, dtype=f32)[:, None, None]
    pmag = jnp.exp(tau * (lr * dt)[None])
    pr = pmag * jnp.cos(tau * (li * dt)[None])
    pi = pmag * jnp.sin(tau * (li * dt)[None])
    cre, cim = c_re.astype(f32), c_im.astype(f32)
    hi = lax.Precision.HIGHEST
    car = cre[None] * pr[:, :, None, :] - cim[None] * pi[:, :, None, :]
    cai = cre[None] * pi[:, :, None, :] + cim[None] * pr[:, :, None, :]
    kern = jnp.einsum('tgok,gki->tgoi', jnp.concatenate([car[:L], -cai[:L]], axis=3),
                      jnp.concatenate([bbr, bbi], axis=1), precision=hi)
    bf = jnp.bfloat16
    K = L * SSM_GROUP
    C = SSM_GROUP
    P = 2 * K
    flat = jnp.concatenate([kern[::-1], jnp.zeros_like(kern)], axis=0).astype(bf)
    flat = flat.transpose(1, 2, 0, 3).reshape(SSM_GROUPS, C, P)
    win = jnp.tile(flat, (1, 1, L + 1))[..., :L * (P + C)].reshape(SSM_GROUPS, C, L, P + C)[..., :K]
    toep_t = win[:, :, ::-1].transpose(0, 2, 1, 3).reshape(SSM_GROUPS, K, K)
    rev_r, rev_i = pr[L - 1::-1][:L], pi[L - 1::-1][:L]
    st_r = rev_r[:, :, :, None] * bbr[None] - rev_i[:, :, :, None] * bbi[None]
    st_i = rev_r[:, :, :, None] * bbi[None] + rev_i[:, :, :, None] * bbr[None]
    w_st = jnp.concatenate([st_r, st_i], axis=2)
    w_st_t = w_st.transpose(1, 2, 0, 3).reshape(SSM_GROUPS, 2 * SSM_STATE, K).astype(bf)
    w_out = jnp.concatenate([car[1:], -cai[1:]], axis=3)
    w_out_t = w_out.transpose(1, 0, 2, 3).reshape(SSM_GROUPS, K, 2 * SSM_STATE).astype(bf)
    a_chunk = jnp.concatenate([pr[L], pi[L]], axis=1)
    return toep_t, w_st_t, w_out_t, a_chunk


def _ssm_in_kernel(*refs):
    L, G, C = SSM_CHUNK, SSM_GROUPS, SSM_GROUP
    n_tiles = SSM_WIDTH // LANES
    u_refs, (w_ref, ut_ref, s_ref) = refs[:n_tiles], refs[n_tiles:]
    rb = ut_ref.shape[2]
    gq = LANES // C
    for q, u_ref in enumerate(u_refs):
        for s in range(L):
            xs = u_ref[pl.ds(s, rb, stride=L), :]
            ut_ref[q * gq:(q + 1) * gq, s * C:(s + 1) * C, :] = (
                xs.T.reshape(gq, C, rb).astype(ut_ref.dtype))

    for g in range(G):
        st = jnp.dot(w_ref[g], ut_ref[g], preferred_element_type=jnp.float32)
        s_ref[:, g, :] = st.T


def _ssm_scan_kernel(s_ref, p_ref, q_ref, o_ref):
    n_chunks = s_ref.shape[1]
    pm, qm = p_ref[...], q_ref[...]

    def body(c, carry):
        h, hs = carry
        o_ref[0, c] = h
        s = s_ref[0, c]
        return (h * pm + hs * qm + s, hs * pm - h * qm + pltpu.roll(s, SSM_STATE, 1))

    zero = jnp.zeros(pm.shape, jnp.float32)
    lax.fori_loop(0, n_chunks, body, (zero, zero), unroll=8)


def _ssm_out_kernel(ut_ref, h_ref, t_ref, w_ref, y_ref, yt_ref):
    L, G, C = SSM_CHUNK, SSM_GROUPS, SSM_GROUP
    rb = ut_ref.shape[2]

    for g in range(G):
        yt = jnp.dot(t_ref[g], ut_ref[g], preferred_element_type=jnp.float32)
        yt += lax.dot_general(w_ref[g], h_ref[:, g, :].astype(jnp.bfloat16), _NT,
                              preferred_element_type=jnp.float32)
        yt_ref[g] = yt
    gq = LANES // C
    for q in range(SSM_WIDTH // LANES):
        for t in range(L):
            z = yt_ref[q * gq:(q + 1) * gq, t * C:(t + 1) * C, :].reshape(LANES, rb)
            y_ref[q, pl.ds(t, rb, stride=L), :] = z.T


def _ssm(u_src, toep_t, w_st_t, w_out_t, a_chunk, B, S, u_tile0=0):
    L, G = SSM_CHUNK, SSM_GROUPS
    nc = S // L
    R = B * nc
    K = L * SSM_GROUP
    N2 = 2 * SSM_STATE
    n_tiles = SSM_WIDTH // LANES
    rb = min(SSM_ROW_BLOCK, R)
    bf = jnp.bfloat16
    ut, s_loc = pl.pallas_call(
        _ssm_in_kernel,
        grid=(R // rb,),
        in_specs=[pl.BlockSpec((rb * L, LANES), functools.partial(lambda q, j: (j, q), u_tile0 + q))
                  for q in range(n_tiles)] + [pl.BlockSpec((G, N2, K), lambda j: (0, 0, 0))],
        out_specs=[pl.BlockSpec((G, K, rb), lambda j: (0, 0, j)),
                   pl.BlockSpec((rb, G, N2), lambda j: (j, 0, 0))],
        out_shape=[jax.ShapeDtypeStruct((G, K, R), bf),
                   jax.ShapeDtypeStruct((R, G, N2), jnp.float32)],
        compiler_params=_params("parallel"),
        name="ssm_in",
    )(*([u_src] * n_tiles), w_st_t)
    s_bc = s_loc.reshape(B, nc, G, N2)
    ar, ai = a_chunk[:, :SSM_STATE], a_chunk[:, SSM_STATE:]
    pm = jnp.concatenate([ar, ar], axis=1)
    qm = jnp.concatenate([-ai, ai], axis=1)
    h_in = pl.pallas_call(
        _ssm_scan_kernel,
        grid=(B,),
        in_specs=[pl.BlockSpec((1, nc, G, N2), lambda b: (b, 0, 0, 0)),
                  pl.BlockSpec((G, N2), lambda b: (0, 0)),
                  pl.BlockSpec((G, N2), lambda b: (0, 0))],
        out_specs=pl.BlockSpec((1, nc, G, N2), lambda b: (b, 0, 0, 0)),
        out_shape=jax.ShapeDtypeStruct((B, nc, G, N2), jnp.float32),
        compiler_params=_params("parallel"),
        name="ssm_scan",
    )(s_bc, pm, qm)
    h_g = h_in.reshape(R, G, N2)
    return pl.pallas_call(
        _ssm_out_kernel,
        grid=(R // rb,),
        in_specs=[pl.BlockSpec((G, K, rb), lambda j: (0, 0, j)),
                  pl.BlockSpec((rb, G, N2), lambda j: (j, 0, 0)),
                  pl.BlockSpec((G, K, K), lambda j: (0, 0, 0)),
                  pl.BlockSpec((G, K, N2), lambda j: (0, 0, 0))],
        out_specs=pl.BlockSpec((n_tiles, rb * L, LANES), lambda j: (0, j, 0)),
        out_shape=jax.ShapeDtypeStruct((n_tiles, B * S, LANES), jnp.float32),
        scratch_shapes=[pltpu.VMEM((G, K, rb), jnp.float32)],
        compiler_params=_params("parallel"),
        name="ssm_out",
    )(ut, h_g, toep_t, w_out_t)


def _route(logits):
    tm = logits.shape[0]
    lane = lax.broadcasted_iota(jnp.int32, (tm, LANES), 1)
    is_g = jnp.logical_and(lane >= N_EXPERTS, lane < N_EXPERTS + N_GROUPS)
    gl = jnp.where(is_g, logits, -jnp.inf)
    gmax = jnp.max(gl, axis=-1, keepdims=True)
    ge = jnp.exp(gl - gmax)
    gp = ge / jnp.sum(ge, axis=-1, keepdims=True)
    g_p = jnp.max(gp, axis=-1, keepdims=True)
    g_lane = jnp.min(jnp.where(jnp.logical_and(is_g, gp == g_p), lane, LANES), axis=-1, keepdims=True)
    g_idx = g_lane - N_EXPERTS
    grp_of_lane = lax.shift_right_logical(lane, int(math.log2(EXPERTS_PER_GROUP)))
    in_grp = jnp.logical_and(lane < N_EXPERTS, grp_of_lane == g_idx)
    el = jnp.where(in_grp, logits, -jnp.inf)
    v1 = jnp.max(el, axis=-1, keepdims=True)
    i1 = jnp.min(jnp.where(el == v1, lane, LANES), axis=-1, keepdims=True)
    el2 = jnp.where(lane == i1, -jnp.inf, el)
    v2 = jnp.max(el2, axis=-1, keepdims=True)
    i2 = jnp.min(jnp.where(el2 == v2, lane, LANES), axis=-1, keepdims=True)
    e2 = jnp.exp(v2 - v1)
    den = 1.0 + e2
    w1 = (1.0 / den) * g_p
    w2 = (e2 / den) * g_p
    return jnp.where(lane == i1, w1, jnp.where(lane == i2, w2, 0.0))


def _merge_kernel(y_ref, u_ref, att_ref, gs_ref, ga_ref, x_ref, d_ref, wglu_ref, bglu_ref,
                  wps_ref, wpa_ref, wo_ref, g2_ref, wr_ref, br_ref,
                  x1_ref, h2_ref, comb_ref):
    bf = jnp.bfloat16
    f32 = jnp.float32
    y_tiles = [y_ref[q] for q in range(y_ref.shape[0])]
    y = jnp.concatenate(y_tiles, axis=1) + d_ref[...] * u_ref[...]
    z = jax.nn.gelu(y)
    gate = jnp.dot(z.astype(bf), wglu_ref[...], preferred_element_type=f32) + bglu_ref[...]
    s5 = z * jax.nn.sigmoid(gate)
    y_ssm = jnp.dot(s5.astype(bf), wps_ref[...], preferred_element_type=f32)
    y_att = jnp.dot(att_ref[...], wpa_ref[...], preferred_element_type=f32)
    merged = jax.nn.sigmoid(gs_ref[...]) * y_ssm + jax.nn.sigmoid(ga_ref[...]) * y_att
    x1 = x_ref[...] + jnp.dot(merged.astype(bf), wo_ref[...], preferred_element_type=f32)
    x1_ref[...] = x1
    ms = jnp.mean(x1 * x1, axis=-1, keepdims=True)
    h2 = x1 * lax.rsqrt(ms + EPS) * g2_ref[...]
    h2_ref[...] = h2.astype(bf)
    logits = _dot_split(h2, wr_ref[...], (((1,), (0,)), ((), ()))) + br_ref[...]
    comb_ref[...] = _route(logits)


def _merge(y_raw, gu, att, x2, d, w_glu, b_glu, w_ps, w_pa, w_o, g2, w_r, b_r, tm=512):
    T, D = x2.shape
    W = SSM_WIDTH
    row = lambda shape, col: pl.BlockSpec(shape, lambda i: (i, col))
    full = lambda a: pl.BlockSpec(a.shape, lambda i: (0,) * a.ndim)
    consts = [d.reshape(1, W), w_glu, b_glu.reshape(1, W), w_ps, w_pa, w_o, g2.reshape(1, D), w_r, b_r]
    return pl.pallas_call(
        _merge_kernel,
        grid=(T // tm,),
        in_specs=[pl.BlockSpec((W // LANES, tm, LANES), lambda i: (0, i, 0)), row((tm, W), 2 * D // W), row((tm, W), 0),
                  row((tm, D), 0), row((tm, D), 1), row((tm, D), 0)] + [full(c) for c in consts],
        out_specs=[row((tm, D), 0), row((tm, D), 0), row((tm, LANES), 0)],
        out_shape=[jax.ShapeDtypeStruct((T, D), jnp.float32),
                   jax.ShapeDtypeStruct((T, D), jnp.bfloat16),
                   jax.ShapeDtypeStruct((T, LANES), jnp.float32)],
        compiler_params=_params("parallel"),
        name="merge",
    )(y_raw, gu, att, gu, gu, x2, *consts)


def _moe_kernel(h_ref, comb_ref, x_ref, wg_ref, wu_ref, wd_ref, o_ref, act_ref):
    f32 = jnp.float32
    h = h_ref[...]
    comb = comb_ref[...]
    step = 4 * D_EXPERT
    for c in range(N_EXPERTS * D_EXPERT // step):
        hg = jnp.dot(h, wg_ref[:, c * step:(c + 1) * step], preferred_element_type=f32)
        hu = jnp.dot(h, wu_ref[:, c * step:(c + 1) * step], preferred_element_type=f32)
        a = jax.nn.silu(hg) * hu
        for j in range(step // D_EXPERT):
            e = c * (step // D_EXPERT) + j
            act_ref[:, e * D_EXPERT:(e + 1) * D_EXPERT] = (
                a[:, j * D_EXPERT:(j + 1) * D_EXPERT] * comb[:, e:e + 1]).astype(act_ref.dtype)
    o_ref[...] = x_ref[...] + jnp.dot(act_ref[...], wd_ref[...], preferred_element_type=f32)


def _moe(h2, comb, x1, wg, wu, wd, tm=512):
    T, D = x1.shape
    F = N_EXPERTS * D_EXPERT
    row = lambda w: pl.BlockSpec((tm, w), lambda i: (i, 0))
    once = lambda a: pl.BlockSpec(a.shape, lambda i: (0, 0), pipeline_mode=pl.Buffered(1))
    return pl.pallas_call(
        _moe_kernel,
        grid=(T // tm,),
        in_specs=[row(D), row(LANES), row(D), once(wg), once(wu), once(wd)],
        out_specs=row(D),
        out_shape=jax.ShapeDtypeStruct((T, D), jnp.float32),
        scratch_shapes=[pltpu.VMEM((tm, F), jnp.bfloat16)],
        compiler_params=_params("parallel"),
        name="moe",
    )(h2, comb, x1, wg, wu, wd)


def _layer(x, norm1_g, w_in, lam_re, lam_im, log_dt, ssm_b_re, ssm_b_im, ssm_c_re, ssm_c_im,
           ssm_d, w_glu, b_glu, q_norm_g, k_norm_g, w_proj_ssm, w_proj_attn, w_out, norm2_g,
           w_router_group, b_router_group, w_router_expert, b_router_expert, w_gate, w_up, w_down):
    B, S, D = x.shape
    bf = jnp.bfloat16
    x2 = x.reshape(B * S, D)
    gu, q, k_bf, v_bf, kmean = _in_proj(x2, norm1_g, w_in.astype(bf), q_norm_g, k_norm_g, B, S)
    att = _moba(q, k_bf, v_bf, kmean.reshape(B, S // MOBA_BLOCK, ATTN_WIDTH))
    toep, w_st, w_o_ssm, a_chunk = _ssm_weights(lam_re, lam_im, log_dt, ssm_b_re, ssm_b_im,
                                               ssm_c_re, ssm_c_im)
    y_raw = _ssm(gu, toep, w_st, w_o_ssm, a_chunk, B, S, u_tile0=2 * D // LANES)
    pad = jnp.zeros((D, LANES - N_EXPERTS - N_GROUPS), jnp.float32)
    w_r = jnp.concatenate([w_router_expert, w_router_group, pad], axis=1)
    b_r = jnp.concatenate([b_router_expert, b_router_group,
                           jnp.zeros((LANES - N_EXPERTS - N_GROUPS,), jnp.float32)]).reshape(1, LANES)
    x1, h2, comb = _merge(y_raw, gu, att.reshape(B * S, ATTN_WIDTH), x2, ssm_d,
                          w_glu.astype(bf), b_glu, w_proj_ssm.astype(bf), w_proj_attn.astype(bf),
                          w_out.astype(bf), norm2_g, w_r, b_r)
    F = N_EXPERTS * D_EXPERT
    wg = w_gate.transpose(1, 0, 2).reshape(D, F).astype(bf)
    wu = w_up.transpose(1, 0, 2).reshape(D, F).astype(bf)
    wd = w_down.reshape(F, D).astype(bf)
    out = _moe(h2, comb, x1, wg, wu, wd)
    return out.reshape(B, S, D)


def kernel(x, norm1_g, w_in, lam_re, lam_im, log_dt, ssm_b_re, ssm_b_im, ssm_c_re, ssm_c_im,
           ssm_d, w_glu, b_glu, q_norm_g, k_norm_g, w_proj_ssm, w_proj_attn, w_out, norm2_g,
           w_router_group, b_router_group, w_router_expert, b_router_expert, w_gate, w_up, w_down):
    args = (norm1_g, w_in, lam_re, lam_im, log_dt, ssm_b_re, ssm_b_im, ssm_c_re, ssm_c_im,
            ssm_d, w_glu, b_glu, q_norm_g, k_norm_g, w_proj_ssm, w_proj_attn, w_out, norm2_g,
            w_router_group, b_router_group, w_router_expert, b_router_expert, w_gate, w_up, w_down)
    for l in range(w_in.shape[0]):
        x = _layer(x, *(a[l] for a in args))
    return x
```

```python
import functools
import math

import jax
import jax.numpy as jnp
from jax import lax
from jax.experimental import pallas as pl
from jax.experimental.pallas import tpu as pltpu

D_MODEL = 1024
SSM_WIDTH = 512
SSM_GROUP = 16
SSM_GROUPS = SSM_WIDTH // SSM_GROUP
SSM_STATE = 64
N_HEADS = 8
HEAD_DIM = 64
ATTN_WIDTH = N_HEADS * HEAD_DIM
MOBA_BLOCK = 256
MOBA_TOPK = 3
ROPE_THETA = 500000.0
ROPE_DIM = HEAD_DIM // 4
N_GROUPS = 4
EXPERTS_PER_GROUP = 8
N_EXPERTS = N_GROUPS * EXPERTS_PER_GROUP
D_EXPERT = 128
EPS = 1e-6
IN_WIDTH = SSM_WIDTH + 3 * ATTN_WIDTH + 2 * D_MODEL

LANES = 128
SUBLANES = 8
VMEM_LIMIT = 56 * 1024 * 1024

SSM_CHUNK = 16
SSM_ROW_BLOCK = 128
NEG_BIG = -1e30
MOBA_KV_CHUNK = 1024
MOBA_SELECT_CHUNK = 1024

_NT = (((1,), (1,)), ((), ()))


def _params(*sem):
    return pltpu.CompilerParams(dimension_semantics=sem, vmem_limit_bytes=VMEM_LIMIT)


def _dot_split(a, b, dims):
    bf, f32 = jnp.bfloat16, jnp.float32
    a_hi, b_hi = a.astype(bf), b.astype(bf)
    a_lo = (a - a_hi.astype(f32)).astype(bf)
    b_lo = (b - b_hi.astype(f32)).astype(bf)
    dot = lambda x, y: lax.dot_general(x, y, dims, preferred_element_type=f32)
    return dot(a_hi, b_hi) + (dot(a_hi, b_lo) + dot(a_lo, b_hi))


def _head_norm_rope(x, g, ones_bd, ct, sa, sb):
    bf, f32 = jnp.bfloat16, jnp.float32
    sq = x * x
    sq_hi = sq.astype(bf)
    sq_lo = (sq - sq_hi.astype(f32)).astype(bf)
    ssq = (jnp.dot(sq_hi, ones_bd, preferred_element_type=f32)
           + jnp.dot(sq_lo, ones_bd, preferred_element_type=f32))
    inv = lax.rsqrt(ssq * (1.0 / HEAD_DIM) + EPS)
    outs = []
    for t in range(x.shape[1] // LANES):
        cols = slice(t * LANES, (t + 1) * LANES)
        y = x[:, cols] * inv[:, cols] * g
        y = y * ct + pltpu.roll(y, LANES - ROPE_DIM // 2, 1) * sa + pltpu.roll(y, ROPE_DIM // 2, 1) * sb
        outs.append(y)
    return jnp.concatenate(outs, axis=1)


def _rope_tables(S):
    half = ROPE_DIM // 2
    f32 = jnp.float32
    d = jnp.arange(LANES) % HEAD_DIM
    rot = d < ROPE_DIM
    inv_freq = jnp.where(rot, ROPE_THETA ** (-(d % half).astype(f32) * 2.0 / ROPE_DIM), 0.0)
    ang = jnp.arange(S, dtype=f32)[:, None] * inv_freq[None, :]
    cos, sin = jnp.cos(ang), jnp.sin(ang)
    ct = jnp.where(rot, cos, 1.0)
    sa = jnp.where(d < half, -sin, 0.0)
    sb = jnp.where(jnp.logical_and(rot, d >= half), sin, 0.0)
    return ct, sa, sb


def _in_proj_kernel(x_ref, g_ref, w_ref, gq_ref, gk_ref, ones_ref, ct_ref, sa_ref, sb_ref,
                    gu_ref, qo_ref, ko_ref, vo_ref, km_ref):
    bf = jnp.bfloat16
    x = x_ref[...]
    ms = jnp.mean(x * x, axis=-1, keepdims=True)
    h = (x * lax.rsqrt(ms + EPS) * g_ref[...]).astype(bf)
    proj = lambda lo, hi: jnp.dot(h, w_ref[:, lo:hi], preferred_element_type=jnp.float32)
    q0 = SSM_WIDTH
    k0, v0, g0 = q0 + ATTN_WIDTH, q0 + 2 * ATTN_WIDTH, q0 + 3 * ATTN_WIDTH
    step = 512
    n_gate = (IN_WIDTH - g0) // step
    for c in range(n_gate):
        gu_ref[:, c * step:(c + 1) * step] = proj(g0 + c * step, g0 + (c + 1) * step)
    gu_ref[:, n_gate * step:] = proj(0, q0)
    ones_bd, ct, sa, sb = ones_ref[...], ct_ref[...], sa_ref[...], sb_ref[...]
    qo_ref[0] = _head_norm_rope(proj(q0, k0), gq_ref[...], ones_bd, ct, sa, sb)
    kn = _head_norm_rope(proj(k0, v0), gk_ref[...], ones_bd, ct, sa, sb)
    ko_ref[0] = kn.astype(bf)
    km_ref[0, 0] = jnp.mean(kn, axis=0, keepdims=True)
    vo_ref[0] = proj(v0, g0).T.astype(bf)


def _in_proj(x2, g, w_bf, gq, gk, B, S):
    ts = MOBA_BLOCK
    nb = S // ts
    T, D = x2.shape
    W = ATTN_WIDTH
    n_gu = IN_WIDTH - 3 * W
    ct, sa, sb = _rope_tables(S)
    head_of = jnp.arange(W) // HEAD_DIM
    ones_bd = (head_of[:, None] == head_of[None, :]).astype(jnp.bfloat16)
    g128 = lambda t: jnp.concatenate([t, t]).reshape(1, LANES)
    const = lambda shape: pl.BlockSpec(shape, lambda b, s: (0,) * len(shape))
    tab = pl.BlockSpec((ts, LANES), lambda b, s: (s, 0))
    tok3 = pl.BlockSpec((1, ts, W), lambda b, s: (b, s, 0))
    return pl.pallas_call(
        _in_proj_kernel,
        grid=(B, nb),
        in_specs=[pl.BlockSpec((ts, D), lambda b, s: (b * nb + s, 0)),
                  const((1, D)), const(w_bf.shape), const((1, LANES)), const((1, LANES)),
                  const((W, W)), tab, tab, tab],
        out_specs=[pl.BlockSpec((ts, n_gu), lambda b, s: (b * nb + s, 0)), tok3, tok3,
                   pl.BlockSpec((1, W, ts), lambda b, s: (b, 0, s)),
                   pl.BlockSpec((1, 1, 1, W), lambda b, s: (b, s, 0, 0))],
        out_shape=[jax.ShapeDtypeStruct((T, n_gu), jnp.float32),
                   jax.ShapeDtypeStruct((B, S, W), jnp.float32),
                   jax.ShapeDtypeStruct((B, S, W), jnp.bfloat16),
                   jax.ShapeDtypeStruct((B, W, S), jnp.bfloat16),
                   jax.ShapeDtypeStruct((B, nb, 1, W), jnp.float32)],
        compiler_params=_params("parallel", "parallel"),
        name="in_proj",
    )(x2, g.reshape(1, D), w_bf, g128(gq), g128(gk), ones_bd, ct, sa, sb)


def _moba_select_kernel(q_ref, km_ref, bias_ref):
    S = q_ref.shape[1]
    nb = km_ref.shape[1]
    f32 = jnp.float32
    km = km_ref[0]
    lane_k = lax.broadcasted_iota(jnp.int32, (nb, LANES), 1)
    km2 = jnp.concatenate([jnp.where(lane_k < HEAD_DIM, km, 0.0),
                           jnp.where(lane_k < HEAD_DIM, 0.0, km)], axis=0)
    tc = MOBA_SELECT_CHUNK
    blk = lax.broadcasted_iota(jnp.int32, (nb, tc), 0)
    col = lax.broadcasted_iota(jnp.int32, (nb, tc), 1)
    shift = int(math.log2(MOBA_BLOCK))
    for c in range(S // tc):
        past = blk < lax.shift_right_logical(col + c * tc, shift)
        scores = _dot_split(km2, q_ref[0, c * tc:(c + 1) * tc, :], _NT)
        for h in range(2):
            s = jnp.where(past, scores[h * nb:(h + 1) * nb], -jnp.inf)
            sel = jnp.zeros((nb, tc), jnp.bool_)
            for _ in range(MOBA_TOPK):
                mx = jnp.max(s, axis=0, keepdims=True)
                first = jnp.min(jnp.where(s == mx, blk, nb), axis=0, keepdims=True)
                pick = blk == first
                sel = jnp.logical_or(sel, jnp.logical_and(pick, past))
                s = jnp.where(pick, -jnp.inf, s)
            bias_t = jnp.where(sel, 0.0, NEG_BIG)
            bias_t = jnp.concatenate([bias_t, jnp.zeros((LANES - nb, tc), f32)], axis=0)
            bias_ref[0, 0, h, c * tc:(c + 1) * tc, :] = bias_t.T.astype(bias_ref.dtype)


def _moba_select(q, kmean):
    B, S, W = q.shape
    nb = S // MOBA_BLOCK
    return pl.pallas_call(
        _moba_select_kernel,
        grid=(B, W // LANES),
        in_specs=[pl.BlockSpec((1, S, LANES), lambda b, hp: (b, 0, hp)),
                  pl.BlockSpec((1, nb, LANES), lambda b, hp: (b, 0, hp))],
        out_specs=pl.BlockSpec((1, 1, 2, S, LANES), lambda b, hp: (b, hp, 0, 0, 0)),
        out_shape=jax.ShapeDtypeStruct((B, W // LANES, 2, S, LANES), jnp.bfloat16),
        compiler_params=_params("parallel", "parallel"),
        name="moba_select",
    )(q, kmean)


def _moba_kernel(q_ref, k_ref, vt_ref, bias_ref, oh_ref, o_ref, s_ref):
    i = pl.program_id(2)
    tq = q_ref.shape[1]
    bf = jnp.bfloat16
    f32 = jnp.float32
    q = q_ref[0]
    lane = lax.broadcasted_iota(jnp.int32, (tq, LANES), 1)
    scale = HEAD_DIM ** -0.5 * math.log2(math.e)

    q_aug = []
    for h in range(2):
        hmask = (lane < HEAD_DIM) if h == 0 else (lane >= HEAD_DIM)
        qh = jnp.where(hmask, q, 0.0)
        q_aug.append(jnp.concatenate([(qh * scale).astype(bf), bias_ref[0, 0, h]], axis=1))

    ck = MOBA_KV_CHUNK
    n_chunks = lax.shift_right_logical(i + (ck // tq - 1), int(math.log2(ck // tq)))
    n_chunks = jnp.maximum(n_chunks, 1)

    def logits(c, slot):
        off = pl.multiple_of(c * ck, ck)
        k_aug = jnp.concatenate([k_ref[0, pl.ds(off, ck), :], oh_ref[pl.ds(off, ck), :]], axis=1)
        cm = []
        for h in range(2):
            s = lax.dot_general(k_aug, q_aug[h], _NT, preferred_element_type=f32)
            s_ref[slot, h] = s
            cm.append(jnp.max(s, axis=0, keepdims=True))
        return cm

    def weigh(s, vt, cm, m, l, acc):
        m_new = cm if m is None else jnp.maximum(m, cm)
        p = jnp.exp2(s - m_new)
        psum = jnp.sum(p, axis=0, keepdims=True)
        pv = jnp.dot(vt, p.astype(bf), preferred_element_type=f32)
        if m is None:
            return [m_new, psum, pv]
        alpha = jnp.exp2(m - m_new)
        return [m_new, alpha * l + psum, alpha * acc + pv]

    def weigh_chunk(c, slot, cm, state):
        off = pl.multiple_of(c * ck, ck)
        out = []
        for h in range(2):
            vt = vt_ref[0, h * HEAD_DIM:(h + 1) * HEAD_DIM, pl.ds(off, ck)]
            out += weigh(s_ref[slot, h], vt, cm[h], *state[3 * h:3 * h + 3])
        return out

    off_d = pl.multiple_of(i * tq, tq)
    kd = k_ref[0, pl.ds(off_d, tq), :]
    key_i = lax.broadcasted_iota(jnp.int32, (tq, tq), 0)
    qry_i = lax.broadcasted_iota(jnp.int32, (tq, tq), 1)
    causal = key_i <= qry_i
    state = []
    for h in range(2):
        s = lax.dot_general(kd, q_aug[h][:, :LANES], _NT, preferred_element_type=f32)
        s = jnp.where(causal, s, -jnp.inf)
        vt = vt_ref[0, h * HEAD_DIM:(h + 1) * HEAD_DIM, pl.ds(off_d, tq)]
        state += weigh(s, vt, jnp.max(s, axis=0, keepdims=True), None, None, None)

    cm0 = logits(0, 0)

    def body(c, carry):
        cm, state = carry[:2], carry[2:]
        state = weigh_chunk(c, c & 1, cm, state)
        cm_next = logits(c + 1, (c + 1) & 1)
        return tuple(cm_next) + tuple(state)

    carry = lax.fori_loop(0, n_chunks - 1, body, tuple(cm0) + tuple(state))
    last = n_chunks - 1
    _, l0, a0, _, l1, a1 = weigh_chunk(last, last & 1, carry[:2], carry[2:])
    o_t = jnp.concatenate([a0 / l0, a1 / l1], axis=0)
    o_ref[0] = o_t.T.astype(o_ref.dtype)


def _moba(q, k_bf, vt_bf, kmean):
    B, S, W = q.shape
    tq = MOBA_BLOCK
    nb = S // tq
    bias = _moba_select(q, kmean)
    onehot = (jnp.arange(S)[:, None] // tq == jnp.arange(LANES)[None, :]).astype(jnp.bfloat16)
    qo = pl.BlockSpec((1, tq, LANES), lambda b, hp, i: (b, i, hp))
    return pl.pallas_call(
        _moba_kernel,
        grid=(B, W // LANES, nb),
        in_specs=[qo,
                  pl.BlockSpec((1, S, LANES), lambda b, hp, i: (b, 0, hp)),
                  pl.BlockSpec((1, LANES, S), lambda b, hp, i: (b, hp, 0)),
                  pl.BlockSpec((1, 1, 2, tq, LANES), lambda b, hp, i: (b, hp, 0, i, 0)),
                  pl.BlockSpec((S, LANES), lambda b, hp, i: (0, 0))],
        out_specs=qo,
        out_shape=jax.ShapeDtypeStruct((B, S, W), jnp.bfloat16),
        scratch_shapes=[pltpu.VMEM((2, 2, MOBA_KV_CHUNK, tq), jnp.float32)],
        compiler_params=_params("parallel", "parallel", "arbitrary"),
        name="moba",
    )(q, k_bf, vt_bf, bias, onehot)


def _ssm_weights(lam_re, lam_im, log_dt, b_re, b_im, c_re, c_im):
    L = SSM_CHUNK
    f32 = jnp.float32
    dt = jnp.exp(log_dt.astype(f32))[:, None]
    lr = jnp.minimum(lam_re.astype(f32), -1e-4)
    li = lam_im.astype(f32)
    mag = jnp.exp(lr * dt)
    ar = mag * jnp.cos(li * dt)
    ai = mag * jnp.sin(li * dt)
    den = lr * lr + li * li
    nr, ni = ar - 1.0, ai
    cr = (nr * lr + ni * li) / den
    ci = (ni * lr - nr * li) / den
    brf, bif = b_re.astype(f32), b_im.astype(f32)
    bbr = cr[..., None] * brf - ci[..., None] * bif
    bbi = cr[..., None] * bif + ci[..., None] * brf
    tau = jnp.arange(L + 1, dtype=f32)[:, None, None]
    pmag = jnp.exp(tau * (lr * dt)[None])
    pr = pmag * jnp.cos(tau * (li * dt)[None])
    pi = pmag * jnp.sin(tau * (li * dt)[None])
    cre, cim = c_re.astype(f32), c_im.astype(f32)
    hi = lax.Precision.HIGHEST
    car = cre[None] * pr[:, :, None, :] - cim[None] * pi[:, :, None, :]
    cai = cre[None] * pi[:, :, None, :] + cim[None] * pr[:, :, None, :]
    kern = jnp.einsum('tgok,gki->tgoi', jnp.concatenate([car[:L], -cai[:L]], axis=3),
                      jnp.concatenate([bbr, bbi], axis=1), precision=hi)
    bf = jnp.bfloat16
    K = L * SSM_GROUP
    C = SSM_GROUP
    P = 2 * K
    flat = jnp.concatenate([kern[::-1], jnp.zeros_like(kern)], axis=0).astype(bf)
    flat = flat.transpose(1, 2, 0, 3).reshape(SSM_GROUPS, C, P)
    win = jnp.tile(flat, (1, 1, L + 1))[..., :L * (P + C)].reshape(SSM_GROUPS, C, L, P + C)[..., :K]
    toep_t = win[:, :, ::-1].transpose(0, 2, 1, 3).reshape(SSM_GROUPS, K, K)
    rev_r, rev_i = pr[L - 1::-1][:L], pi[L - 1::-1][:L]
    st_r = rev_r[:, :, :, None] * bbr[None] - rev_i[:, :, :, None] * bbi[None]
    st_i = rev_r[:, :, :, None] * bbi[None] + rev_i[:, :, :, None] * bbr[None]
    w_st = jnp.concatenate([st_r, st_i], axis=2)
    w_st_t = w_st.transpose(1, 2, 0, 3).reshape(SSM_GROUPS, 2 * SSM_STATE, K).astype(bf)
    w_out = jnp.concatenate([car[1:], -cai[1:]], axis=3)
    w_out_t = w_out.transpose(1, 0, 2, 3).reshape(SSM_GROUPS, K, 2 * SSM_STATE).astype(bf)
    a_chunk = jnp.concatenate([pr[L], pi[L]], axis=1)
    return toep_t, w_st_t, w_out_t, a_chunk


def _ssm_in_kernel(*refs):
    L, G, C = SSM_CHUNK, SSM_GROUPS, SSM_GROUP
    n_tiles = SSM_WIDTH // LANES
    u_refs, (w_ref, ut_ref, s_ref) = refs[:n_tiles], refs[n_tiles:]
    rb = ut_ref.shape[2]
    gq = LANES // C
    for q, u_ref in enumerate(u_refs):
        for s in range(L):
            xs = u_ref[pl.ds(s, rb, stride=L), :]
            ut_ref[q * gq:(q + 1) * gq, s * C:(s + 1) * C, :] = (
                xs.T.reshape(gq, C, rb).astype(ut_ref.dtype))

    for g in range(G):
        st = jnp.dot(w_ref[g], ut_ref[g], preferred_element_type=jnp.float32)
        s_ref[:, g, :] = st.T


def _ssm_scan_kernel(s_ref, p_ref, q_ref, o_ref):
    n_chunks = s_ref.shape[1]
    pm, qm = p_ref[...], q_ref[...]

    def body(c, carry):
        h, hs = carry
        o_ref[0, c] = h
        s = s_ref[0, c]
        return (h * pm + hs * qm + s, hs * pm - h * qm + pltpu.roll(s, SSM_STATE, 1))

    zero = jnp.zeros(pm.shape, jnp.float32)
    lax.fori_loop(0, n_chunks, body, (zero, zero), unroll=8)


def _ssm_out_kernel(ut_ref, h_ref, t_ref, w_ref, y_ref, yt_ref):
    L, G, C = SSM_CHUNK, SSM_GROUPS, SSM_GROUP
    rb = ut_ref.shape[2]

    for g in range(G):
        yt = jnp.dot(t_ref[g], ut_ref[g], preferred_element_type=jnp.float32)
        yt += lax.dot_general(w_ref[g], h_ref[:, g, :].astype(jnp.bfloat16), _NT,
                              preferred_element_type=jnp.float32)
        yt_ref[g] = yt
    gq = LANES // C
    for q in range(SSM_WIDTH // LANES):
        for t in range(L):
            z = yt_ref[q * gq:(q + 1) * gq, t * C:(t + 1) * C, :].reshape(LANES, rb)
            y_ref[q, pl.ds(t, rb, stride=L), :] = z.T


def _ssm(u_src, toep_t, w_st_t, w_out_t, a_chunk, B, S, u_tile0=0):
    L, G = SSM_CHUNK, SSM_GROUPS
    nc = S // L
    R = B * nc
    K = L * SSM_GROUP
    N2 = 2 * SSM_STATE
    n_tiles = SSM_WIDTH // LANES
    rb = min(SSM_ROW_BLOCK, R)
    bf = jnp.bfloat16
    ut, s_loc = pl.pallas_call(
        _ssm_in_kernel,
        grid=(R // rb,),
        in_specs=[pl.BlockSpec((rb * L, LANES), functools.partial(lambda q, j: (j, q), u_tile0 + q))
                  for q in range(n_tiles)] + [pl.BlockSpec((G, N2, K), lambda j: (0, 0, 0))],
        out_specs=[pl.BlockSpec((G, K, rb), lambda j: (0, 0, j)),
                   pl.BlockSpec((rb, G, N2), lambda j: (j, 0, 0))],
        out_shape=[jax.ShapeDtypeStruct((G, K, R), bf),
                   jax.ShapeDtypeStruct((R, G, N2), jnp.float32)],
        compiler_params=_params("parallel"),
        name="ssm_in",
    )(*([u_src] * n_tiles), w_st_t)
    s_bc = s_loc.reshape(B, nc, G, N2)
    ar, ai = a_chunk[:, :SSM_STATE], a_chunk[:, SSM_STATE:]
    pm = jnp.concatenate([ar, ar], axis=1)
    qm = jnp.concatenate([-ai, ai], axis=1)
    h_in = pl.pallas_call(
        _ssm_scan_kernel,
        grid=(B,),
        in_specs=[pl.BlockSpec((1, nc, G, N2), lambda b: (b, 0, 0, 0)),
                  pl.BlockSpec((G, N2), lambda b: (0, 0)),
                  pl.BlockSpec((G, N2), lambda b: (0, 0))],
        out_specs=pl.BlockSpec((1, nc, G, N2), lambda b: (b, 0, 0, 0)),
        out_shape=jax.ShapeDtypeStruct((B, nc, G, N2), jnp.float32),
        compiler_params=_params("parallel"),
        name="ssm_scan",
    )(s_bc, pm, qm)
    h_g = h_in.reshape(R, G, N2)
    return pl.pallas_call(
        _ssm_out_kernel,
        grid=(R // rb,),
        in_specs=[pl.BlockSpec((G, K, rb), lambda j: (0, 0, j)),
                  pl.BlockSpec((rb, G, N2), lambda j: (j, 0, 0)),
                  pl.BlockSpec((G, K, K), lambda j: (0, 0, 0)),
                  pl.BlockSpec((G, K, N2), lambda j: (0, 0, 0))],
        out_specs=pl.BlockSpec((n_tiles, rb * L, LANES), lambda j: (0, j, 0)),
        out_shape=jax.ShapeDtypeStruct((n_tiles, B * S, LANES), jnp.float32),
        scratch_shapes=[pltpu.VMEM((G, K, rb), jnp.float32)],
        compiler_params=_params("parallel"),
        name="ssm_out",
    )(ut, h_g, toep_t, w_out_t)


def _route(logits):
    tm = logits.shape[0]
    lane = lax.broadcasted_iota(jnp.int32, (tm, LANES), 1)
    is_g = jnp.logical_and(lane >= N_EXPERTS, lane < N_EXPERTS + N_GROUPS)
    gl = jnp.where(is_g, logits, -jnp.inf)
    gmax = jnp.max(gl, axis=-1, keepdims=True)
    ge = jnp.exp(gl - gmax)
    gp = ge / jnp.sum(ge, axis=-1, keepdims=True)
    g_p = jnp.max(gp, axis=-1, keepdims=True)
    g_lane = jnp.min(jnp.where(jnp.logical_and(is_g, gp == g_p), lane, LANES), axis=-1, keepdims=True)
    g_idx = g_lane - N_EXPERTS
    grp_of_lane = lax.shift_right_logical(lane, int(math.log2(EXPERTS_PER_GROUP)))
    in_grp = jnp.logical_and(lane < N_EXPERTS, grp_of_lane == g_idx)
    el = jnp.where(in_grp, logits, -jnp.inf)
    v1 = jnp.max(el, axis=-1, keepdims=True)
    i1 = jnp.min(jnp.where(el == v1, lane, LANES), axis=-1, keepdims=True)
    el2 = jnp.where(lane == i1, -jnp.inf, el)
    v2 = jnp.max(el2, axis=-1, keepdims=True)
    i2 = jnp.min(jnp.where(el2 == v2, lane, LANES), axis=-1, keepdims=True)
    e2 = jnp.exp(v2 - v1)
    den = 1.0 + e2
    w1 = (1.0 / den) * g_p
    w2 = (e2 / den) * g_p
    return jnp.where(lane == i1, w1, jnp.where(lane == i2, w2, 0.0))


def _merge_kernel(y_ref, u_ref, att_ref, gs_ref, ga_ref, x_ref, d_ref, wglu_ref, bglu_ref,
                  wps_ref, wpa_ref, wo_ref, g2_ref, wr_ref, br_ref,
                  x1_ref, h2_ref, comb_ref):
    bf = jnp.bfloat16
    f32 = jnp.float32
    y_tiles = [y_ref[q] for q in range(y_ref.shape[0])]
    y = jnp.concatenate(y_tiles, axis=1) + d_ref[...] * u_ref[...]
    z = jax.nn.gelu(y)
    gate = jnp.dot(z.astype(bf), wglu_ref[...], preferred_element_type=f32) + bglu_ref[...]
    s5 = z * jax.nn.sigmoid(gate)
    y_ssm = jnp.dot(s5.astype(bf), wps_ref[...], preferred_element_type=f32)
    y_att = jnp.dot(att_ref[...], wpa_ref[...], preferred_element_type=f32)
    merged = jax.nn.sigmoid(gs_ref[...]) * y_ssm + jax.nn.sigmoid(ga_ref[...]) * y_att
    x1 = x_ref[...] + jnp.dot(merged.astype(bf), wo_ref[...], preferred_element_type=f32)
    x1_ref[...] = x1
    ms = jnp.mean(x1 * x1, axis=-1, keepdims=True)
    h2 = x1 * lax.rsqrt(ms + EPS) * g2_ref[...]
    h2_ref[...] = h2.astype(bf)
    logits = _dot_split(h2, wr_ref[...], (((1,), (0,)), ((), ()))) + br_ref[...]
    comb_ref[...] = _route(logits)


def _merge(y_raw, gu, att, x2, d, w_glu, b_glu, w_ps, w_pa, w_o, g2, w_r, b_r, tm=512):
    T, D = x2.shape
    W = SSM_WIDTH
    row = lambda shape, col: pl.BlockSpec(shape, lambda i: (i, col))
    full = lambda a: pl.BlockSpec(a.shape, lambda i: (0,) * a.ndim)
    consts = [d.reshape(1, W), w_glu, b_glu.reshape(1, W), w_ps, w_pa, w_o, g2.reshape(1, D), w_r, b_r]
    return pl.pallas_call(
        _merge_kernel,
        grid=(T // tm,),
        in_specs=[pl.BlockSpec((W // LANES, tm, LANES), lambda i: (0, i, 0)), row((tm, W), 2 * D // W), row((tm, W), 0),
                  row((tm, D), 0), row((tm, D), 1), row((tm, D), 0)] + [full(c) for c in consts],
        out_specs=[row((tm, D), 0), row((tm, D), 0), row((tm, LANES), 0)],
        out_shape=[jax.ShapeDtypeStruct((T, D), jnp.float32),
                   jax.ShapeDtypeStruct((T, D), jnp.bfloat16),
                   jax.ShapeDtypeStruct((T, LANES), jnp.float32)],
        compiler_params=_params("parallel"),
        name="merge",
    )(y_raw, gu, att, gu, gu, x2, *consts)


def _moe_kernel(h_ref, comb_ref, x_ref, wg_ref, wu_ref, wd_ref, o_ref, act_ref):
    f32 = jnp.float32
    h = h_ref[...]
    comb = comb_ref[...]
    step = 4 * D_EXPERT
    for c in range(N_EXPERTS * D_EXPERT // step):
        hg = jnp.dot(h, wg_ref[:, c * step:(c + 1) * step], preferred_element_type=f32)
        hu = jnp.dot(h, wu_ref[:, c * step:(c + 1) * step], preferred_element_type=f32)
        a = jax.nn.silu(hg) * hu
        for j in range(step // D_EXPERT):
            e = c * (step // D_EXPERT) + j
            act_ref[:, e * D_EXPERT:(e + 1) * D_EXPERT] = (
                a[:, j * D_EXPERT:(j + 1) * D_EXPERT] * comb[:, e:e + 1]).astype(act_ref.dtype)
    o_ref[...] = x_ref[...] + jnp.dot(act_ref[...], wd_ref[...], preferred_element_type=f32)


def _moe(h2, comb, x1, wg, wu, wd, tm=512):
    T, D = x1.shape
    F = N_EXPERTS * D_EXPERT
    row = lambda w: pl.BlockSpec((tm, w), lambda i: (i, 0))
    once = lambda a: pl.BlockSpec(a.shape, lambda i: (0, 0), pipeline_mode=pl.Buffered(1))
    return pl.pallas_call(
        _moe_kernel,
        grid=(T // tm,),
        in_specs=[row(D), row(LANES), row(D), once(wg), once(wu), once(wd)],
        out_specs=row(D),
        out_shape=jax.ShapeDtypeStruct((T, D), jnp.float32),
        scratch_shapes=[pltpu.VMEM((tm, F), jnp.bfloat16)],
        compiler_params=_params("parallel"),
        name="moe",
    )(h2, comb, x1, wg, wu, wd)


def _layer(x, norm1_g, w_in, lam_re, lam_im, log_dt, ssm_b_re, ssm_b_im, ssm_c_re, ssm_c_im,
           ssm_d, w_glu, b_glu, q_norm_g, k_norm_g, w_proj_ssm, w_proj_attn, w_out, norm2_g,
           w_router_group, b_router_group, w_router_expert, b_router_expert, w_gate, w_up, w_down):
    B, S, D = x.shape
    bf = jnp.bfloat16
    x2 = x.reshape(B * S, D)
    gu, q, k_bf, v_bf, kmean = _in_proj(x2, norm1_g, w_in.astype(bf), q_norm_g, k_norm_g, B, S)
    att = _moba(q, k_bf, v_bf, kmean.reshape(B, S // MOBA_BLOCK, ATTN_WIDTH))
    toep, w_st, w_o_ssm, a_chunk = _ssm_weights(lam_re, lam_im, log_dt, ssm_b_re, ssm_b_im,
                                               ssm_c_re, ssm_c_im)
    y_raw = _ssm(gu, toep, w_st, w_o_ssm, a_chunk, B, S, u_tile0=2 * D // LANES)
    pad = jnp.zeros((D, LANES - N_EXPERTS - N_GROUPS), jnp.float32)
    w_r = jnp.concatenate([w_router_expert, w_router_group, pad], axis=1)
    b_r = jnp.concatenate([b_router_expert, b_router_group,
                           jnp.zeros((LANES - N_EXPERTS - N_GROUPS,), jnp.float32)]).reshape(1, LANES)
    x1, h2, comb = _merge(y_raw, gu, att.reshape(B * S, ATTN_WIDTH), x2, ssm_d,
                          w_glu.astype(bf), b_glu, w_proj_ssm.astype(bf), w_proj_attn.astype(bf),
                          w_out.astype(bf), norm2_g, w_r, b_r)
    F = N_EXPERTS * D_EXPERT
    wg = w_gate.transpose(1, 0, 2).reshape(D, F).astype(bf)
    wu = w_up.transpose(1, 0, 2).reshape(D, F).astype(bf)
    wd = w_down.reshape(F, D).astype(bf)
    out = _moe(h2, comb, x1, wg, wu, wd)
    return out.reshape(B, S, D)


def kernel(x, norm1_g, w_in, lam_re, lam_im, log_dt, ssm_b_re, ssm_b_im, ssm_c_re, ssm_c_im,
           ssm_d, w_glu, b_glu, q_norm_g, k_norm_g, w_proj_ssm, w_proj_attn, w_out, norm2_g,
           w_router_group, b_router_group, w_router_expert, b_router_expert, w_gate, w_up, w_down):
    args = (norm1_g, w_in, lam_re, lam_im, log_dt, ssm_b_re, ssm_b_im, ssm_c_re, ssm_c_im,
            ssm_d, w_glu, b_glu, q_norm_g, k_norm_g, w_proj_ssm, w_proj_attn, w_out, norm2_g,
            w_router_group, b_router_group, w_router_expert, b_router_expert, w_gate, w_up, w_down)
    for l in range(w_in.shape[0]):
        x = _layer(x, *(a[l] for a in args))
    return x
```
